```python
import jax, jax.numpy as jnp
from jax import lax
import numpy as np

D_MODEL = 1024
BATCH = 8
SEQ = 2048
DEPTH = 1
DEC_BATCH = 8
DEC_SEQ = 8192
PAST_LEN = 128

HEAD_DIM = 64
A_HEADS = 8
A_KV_HEADS = 2
A_WINDOW = 128
B_GROUPS = ((128, 1), (512, 4), (2048, 16))
B_HEADS_PER_GROUP = 4
B_HEADS = B_HEADS_PER_GROUP * len(B_GROUPS)
N_BIAS_HEADS = A_HEADS + B_HEADS
REL_BUCKETS = 32
REL_MAX_DISTANCE = 1024
D_FF = 4 * D_MODEL
A_Q = A_HEADS * HEAD_DIM
A_KV = A_KV_HEADS * HEAD_DIM
B_W = B_HEADS * HEAD_DIM
B_OUT = B_HEADS_PER_GROUP * HEAD_DIM
IN_COLS = A_Q + 2 * A_KV + 3 * B_W + 2 * D_MODEL
RMS_EPS = 1e-6
NEG_INF = -1e30

kernel_name = 'hybrid_gated_window_dilated_encoder'


def _rmsnorm(x, g):
    x32 = x.astype(jnp.float32)
    y = x32 * lax.rsqrt(jnp.mean(x32 * x32, axis=-1, keepdims=True) + RMS_EPS)
    return (y * g.astype(jnp.float32)).astype(x.dtype)


def _rel_bucket(rel):
    half = REL_BUCKETS // 2
    max_exact = half // 2
    n = np.abs(rel)
    large = max_exact + (np.log(np.maximum(n, 1) / max_exact)
                         / np.log(REL_MAX_DISTANCE / max_exact) * (half - max_exact)).astype(np.int32)
    large = np.minimum(large, half - 1)
    return (np.where(rel > 0, half, 0) + np.where(n < max_exact, n, large)).astype(np.int32)


def _banded_attention(q, k, v, n, dist_scale, bias_table, sink):
    b, L, H, dh = q.shape
    G = k.shape[2]
    r = H // G
    nb = -(-L // n)
    Lp = nb * n
    qb = jnp.pad(q, ((0, 0), (0, Lp - L), (0, 0), (0, 0))).reshape(b, nb, n, G, r, dh)
    kv_pad = ((0, 0), (n, Lp - L + n), (0, 0), (0, 0))
    kp = jnp.pad(k, kv_pad).reshape(b, nb + 2, n, G, dh)
    vp = jnp.pad(v, kv_pad).reshape(b, nb + 2, n, G, dh)
    kb = jnp.concatenate([kp[:, :-2], kp[:, 1:-1], kp[:, 2:]], axis=2)
    vb = jnp.concatenate([vp[:, :-2], vp[:, 1:-1], vp[:, 2:]], axis=2)
    s = jnp.einsum('bnqgrd,bnkgd->bngrqk', qb, kb).astype(jnp.float32) * (dh ** -0.5)
    rel = np.arange(3 * n)[None, :] - n - np.arange(n)[:, None]
    bias = bias_table.astype(jnp.float32)[_rel_bucket(rel * dist_scale)]
    bias = jnp.transpose(bias.reshape(n, 3 * n, G, r), (2, 3, 0, 1))
    kpos = np.arange(nb)[:, None] * n + np.arange(3 * n)[None, :] - n
    valid = (np.abs(rel) <= n)[None] & ((kpos >= 0) & (kpos < L))[:, None, :]
    logits = jnp.where(valid[None, :, None, None], s + bias, NEG_INF)
    m = jnp.max(logits, axis=-1)
    if sink is not None:
        sink_r = sink.astype(jnp.float32).reshape(G, r)[..., None]
        m = jnp.maximum(m, sink_r)
    p = jnp.exp(logits - m[..., None])
    denom = jnp.sum(p, axis=-1)
    if sink is not None:
        denom = denom + jnp.exp(sink_r - m)
    o = jnp.einsum('bngrqk,bnkgd->bnqgrd', p.astype(vb.dtype), vb).astype(jnp.float32)
    o = o / jnp.transpose(denom, (0, 1, 4, 2, 3))[..., None]
    o = o.reshape(b, Lp, H, dh)[:, :L].astype(q.dtype)
    lse = jnp.transpose(m + jnp.log(denom), (0, 1, 4, 2, 3)).reshape(b, Lp, H)[:, :L]
    return o, lse


def _to_residues(t, dil):
    b, S = t.shape[:2]
    t = jnp.moveaxis(t.reshape((b, S // dil, dil) + t.shape[2:]), 2, 1)
    return t.reshape((b * dil, S // dil) + t.shape[3:])


def _from_residues(t, dil):
    bd, L = t.shape[:2]
    t = jnp.moveaxis(t.reshape((bd // dil, dil, L) + t.shape[2:]), 1, 2)
    return t.reshape((bd // dil, L * dil) + t.shape[3:])


def _dilated_mixture(q, k, v, bias_b):
    b, S, _, dh = q.shape
    outs, lses = [], []
    for gi, (window, dil) in enumerate(B_GROUPS):
        hs = slice(gi * B_HEADS_PER_GROUP, (gi + 1) * B_HEADS_PER_GROUP)
        o, lse = _banded_attention(_to_residues(q[:, :, hs], dil), _to_residues(k[:, :, hs], dil),
                                   _to_residues(v[:, :, hs], dil), window // (2 * dil), dil,
                                   bias_b[:, hs], None)
        outs.append(_from_residues(o, dil))
        lses.append(_from_residues(lse, dil))
    alpha = jax.nn.softmax(jnp.stack(lses), axis=0)
    o = jnp.sum(alpha[..., None] * jnp.stack(outs).astype(jnp.float32), axis=0)
    return o.astype(q.dtype).reshape(b, S, B_OUT)


def _layer(x, rel_bias, g_mix, w_in, b_gate, w_branch_a, w_branch_b, w_out, sink, g_mlp, w_up, w_down):
    b, S, _ = x.shape
    h = _rmsnorm(x, g_mix)
    z = h @ w_in
    o1 = A_Q
    o2 = o1 + A_KV
    o3 = o2 + A_KV
    o4 = o3 + B_W
    o5 = o4 + B_W
    o6 = o5 + B_W
    qa = z[..., :o1].reshape(b, S, A_HEADS, HEAD_DIM)
    ka = z[..., o1:o2].reshape(b, S, A_KV_HEADS, HEAD_DIM)
    va = z[..., o2:o3].reshape(b, S, A_KV_HEADS, HEAD_DIM)
    qb = z[..., o3:o4].reshape(b, S, B_HEADS, HEAD_DIM)
    kb = z[..., o4:o5].reshape(b, S, B_HEADS, HEAD_DIM)
    vb = z[..., o5:o6].reshape(b, S, B_HEADS, HEAD_DIM)
    gates = jax.nn.sigmoid((z[..., o6:] + b_gate).astype(jnp.float32)).astype(x.dtype)
    gates = gates.reshape(b, S, 2, D_MODEL)
    o_a, _ = _banded_attention(qa, ka, va, A_WINDOW, 1, rel_bias[:, :A_HEADS], sink)
    o_b = _dilated_mixture(qb, kb, vb, rel_bias[:, A_HEADS:])
    merged = gates[:, :, 0] * (o_a.reshape(b, S, A_Q) @ w_branch_a) + gates[:, :, 1] * (o_b @ w_branch_b)
    x = x + merged @ w_out
    h = _rmsnorm(x, g_mlp)
    return x + jnp.square(jax.nn.relu(h @ w_up)) @ w_down


def _encoder(x, rel_bias, g_mix, w_in, b_gate, w_branch_a, w_branch_b, w_out, attn_sink, g_mlp, w_up, w_down, g_final):
    for l in range(DEPTH):
        x = _layer(x, rel_bias, g_mix[l], w_in[l], b_gate[l], w_branch_a[l], w_branch_b[l], w_out[l],
                   attn_sink[l], g_mlp[l], w_up[l], w_down[l])
    return _rmsnorm(x, g_final)


def setup_inputs(seed: int = 0) -> dict:
    key = jax.random.key(seed)
    ks = jax.random.split(key, 14)

    def nrm(k, shape, scale):
        return scale * jax.random.normal(k, shape, jnp.float32)

    return {
        'x_prompt': nrm(ks[0], (BATCH, SEQ, D_MODEL), 1.0),
        'x_sample': nrm(ks[1], (DEC_BATCH, DEC_SEQ, D_MODEL), 1.0),
        'rel_bias': nrm(ks[2], (REL_BUCKETS, N_BIAS_HEADS), 0.5),
        'g_mix': 1.0 + nrm(ks[3], (DEPTH, D_MODEL), 0.05),
        'w_in': nrm(ks[4], (DEPTH, D_MODEL, IN_COLS), D_MODEL ** -0.5),
        'b_gate': nrm(ks[5], (DEPTH, 2 * D_MODEL), 0.1),
        'w_branch_a': nrm(ks[6], (DEPTH, A_Q, D_MODEL), A_Q ** -0.5),
        'w_branch_b': nrm(ks[7], (DEPTH, B_OUT, D_MODEL), B_OUT ** -0.5),
        'w_out': nrm(ks[8], (DEPTH, D_MODEL, D_MODEL), D_MODEL ** -0.5),
        'attn_sink': nrm(ks[9], (DEPTH, A_HEADS), 0.5),
        'g_mlp': 1.0 + nrm(ks[10], (DEPTH, D_MODEL), 0.05),
        'w_up': nrm(ks[11], (DEPTH, D_MODEL, D_FF), D_MODEL ** -0.5),
        'w_down': nrm(ks[12], (DEPTH, D_FF, D_MODEL), D_FF ** -0.5),
        'g_final': 1.0 + nrm(ks[13], (D_MODEL,), 0.05),
    }


def reference(x_prompt, x_sample, rel_bias, g_mix, w_in, b_gate, w_branch_a, w_branch_b, w_out,
              attn_sink, g_mlp, w_up, w_down, g_final):
    y_prompt = _encoder(x_prompt, rel_bias, g_mix, w_in, b_gate, w_branch_a, w_branch_b, w_out,
                        attn_sink, g_mlp, w_up, w_down, g_final)
    y_sample = _encoder(x_sample, rel_bias, g_mix, w_in, b_gate, w_branch_a, w_branch_b, w_out,
                        attn_sink, g_mlp, w_up, w_down, g_final)
    return (y_prompt, y_sample)
```

```python
import functools

import numpy as np
import jax
import jax.numpy as jnp
from jax import lax
from jax.experimental import pallas as pl
from jax.experimental.pallas import tpu as pltpu

D_MODEL = 1024
HEAD_DIM = 64
A_HEADS = 8
A_KV_HEADS = 2
A_WINDOW = 128
B_GROUPS = ((128, 1), (512, 4), (2048, 16))
B_HEADS_PER_GROUP = 4
B_HEADS = B_HEADS_PER_GROUP * len(B_GROUPS)
REL_BUCKETS = 32
REL_MAX_DISTANCE = 1024
D_FF = 4 * D_MODEL
A_Q = A_HEADS * HEAD_DIM
A_KV = A_KV_HEADS * HEAD_DIM
B_W = B_HEADS * HEAD_DIM
B_OUT = B_HEADS_PER_GROUP * HEAD_DIM
RMS_EPS = 1e-6
NEG_INF = -1e30

QUAD = 4 * HEAD_DIM
SLAB = 3 * QUAD
Q_BLOCK = 128
VMEM_LIMIT_BYTES = 58 * 1024 * 1024

A_HEAD_ORDER = (0, 4, 1, 5, 2, 6, 3, 7)

_BF16 = jnp.bfloat16
_F32 = jnp.float32


def _rel_bucket(rel):
    half = REL_BUCKETS // 2
    max_exact = half // 2
    n = np.abs(rel)
    large = max_exact + (np.log(np.maximum(n, 1) / max_exact)
                         / np.log(REL_MAX_DISTANCE / max_exact) * (half - max_exact)).astype(np.int32)
    large = np.minimum(large, half - 1)
    return (np.where(rel > 0, half, 0) + np.where(n < max_exact, n, large)).astype(np.int32)


def _bias_blocks(table, n, dist_scale, nk):
    rel = np.arange(nk)[None, :] - n - np.arange(Q_BLOCK)[:, None]
    bias = jnp.transpose(table.astype(_F32)[_rel_bucket(rel * dist_scale)], (2, 0, 1))
    band = np.abs(rel) <= n
    col = np.arange(nk)[None, :]
    variants = []
    for var in range(4):
        ok = band.copy()
        if var & 1:
            ok &= col >= n
        if var & 2:
            ok &= col < nk - n
        variants.append(jnp.where(ok[None], bias, NEG_INF))
    return jnp.stack(variants)


def _inproj_kernel(x_ref, g_ref, w_ref, b_ref, za_ref, zb1_ref, zb2_ref, zb3_ref, gate_ref):
    x = x_ref[...]
    ms = jnp.mean(x * x, axis=-1, keepdims=True)
    h = (x * lax.rsqrt(ms + RMS_EPS) * g_ref[...]).astype(_BF16)
    for i, out in enumerate((za_ref, zb1_ref, zb2_ref, zb3_ref)):
        out[...] = jnp.dot(h, w_ref[:, i * SLAB:(i + 1) * SLAB],
                           preferred_element_type=_F32).astype(_BF16)
    zg = jnp.dot(h, w_ref[:, 4 * SLAB:], preferred_element_type=_F32) + b_ref[...]
    gate_ref[...] = jax.nn.sigmoid(zg).astype(_BF16)


def _inproj(x2d, g_mix, w_cat, b_gate, tm):
    t = x2d.shape[0]
    const = lambda i: (0, 0)
    row = lambda i: (i, 0)
    slab = jax.ShapeDtypeStruct((t, SLAB), _BF16)
    return pl.pallas_call(
        _inproj_kernel,
        grid=(t // tm,),
        in_specs=[
            pl.BlockSpec((tm, D_MODEL), row),
            pl.BlockSpec((1, D_MODEL), const),
            pl.BlockSpec(w_cat.shape, const, pipeline_mode=pl.Buffered(1)),
            pl.BlockSpec((1, 2 * D_MODEL), const),
        ],
        out_specs=[pl.BlockSpec((tm, SLAB), row)] * 4 + [pl.BlockSpec((tm, 2 * D_MODEL), row)],
        out_shape=[slab] * 4 + [jax.ShapeDtypeStruct((t, 2 * D_MODEL), _BF16)],
        compiler_params=pltpu.CompilerParams(
            dimension_semantics=("arbitrary",), vmem_limit_bytes=VMEM_LIMIT_BYTES),
        name="inproj",
    )(x2d, g_mix, w_cat, b_gate)


def _attn_kernel(*refs, halo, lc, nquad, krep, has_sink, want_lse, chunk_axis):
    it = iter(refs)
    q_ref, kc_ref, vc_ref, kp_ref, vp_ref, kn_ref, vn_ref, bias_ref = (next(it) for _ in range(8))
    sink_ref = next(it) if has_sink else None
    o_ref = next(it)
    lse_ref = next(it) if want_lse else None
    kext_ref, vm_ref = next(it), next(it)

    nk = Q_BLOCK + 2 * halo
    nblk = lc // Q_BLOCK
    c = pl.program_id(chunk_axis)
    last_c = pl.num_programs(chunk_axis) - 1

    def rep(v):
        return v if krep == 1 else jnp.concatenate([v] * krep, axis=-1)

    pieces = ((0, halo, kp_ref, vp_ref), (halo, lc, kc_ref, vc_ref), (halo + lc, halo, kn_ref, vn_ref))
    for lo, rows, k_src, v_src in pieces:
        kext_ref[lo:lo + rows, :] = rep(k_src[...])
        v = rep(v_src[...])
        grp = lax.broadcasted_iota(jnp.int32, v.shape, 1) // HEAD_DIM
        for j in range(4):
            vm_ref[j, lo:lo + rows, :] = jnp.where(grp == j, v, jnp.zeros_like(v))

    qgrp = lax.broadcasted_iota(jnp.int32, (Q_BLOCK, QUAD), 1) // HEAD_DIM

    def by_head(cols):
        out = jnp.broadcast_to(cols[3], (Q_BLOCK, QUAD))
        for j in (2, 1, 0):
            out = jnp.where(qgrp == j, cols[j], out)
        return out

    def block(i, carry):
        r0 = pl.multiple_of(i * Q_BLOCK, Q_BLOCK)
        var = (jnp.where((c == 0) & (i == 0), 1, 0)
               + jnp.where((c == last_c) & (i == nblk - 1), 2, 0))
        kk = kext_ref[pl.ds(r0, nk), :]
        vcat = jnp.concatenate([vm_ref[j, pl.ds(r0, nk), :] for j in range(4)], axis=0)
        for quad in range(nquad):
            q4 = q_ref[pl.ds(r0, Q_BLOCK), quad * QUAD:(quad + 1) * QUAD]
            qs = jnp.concatenate(
                [jnp.where(qgrp == j, q4, jnp.zeros_like(q4)) for j in range(4)], axis=0)
            s = lax.dot_general(qs, kk, (((1,), (1,)), ((), ())), preferred_element_type=_F32)
            ps, dens, maxes = [], [], []
            for j in range(4):
                head = quad * 4 + j
                logits = s[j * Q_BLOCK:(j + 1) * Q_BLOCK, :] + bias_ref[var, head]
                m = jnp.max(logits, axis=-1, keepdims=True)
                if has_sink:
                    sink = sink_ref[head]
                    m = jnp.maximum(m, sink)
                p = jnp.exp(logits - m)
                den = jnp.sum(p, axis=-1, keepdims=True)
                if has_sink:
                    den = den + jnp.exp(sink - m)
                ps.append(p.astype(_BF16))
                dens.append(den)
                maxes.append(m)
            o = jnp.dot(jnp.concatenate(ps, axis=-1), vcat, preferred_element_type=_F32)
            o_ref[pl.ds(r0, Q_BLOCK), quad * QUAD:(quad + 1) * QUAD] = (o / by_head(dens)).astype(_BF16)
            if want_lse:
                lse_ref[pl.ds(r0, Q_BLOCK), :] = by_head(
                    [m + jnp.log(d) for m, d in zip(maxes, dens)])
        return carry

    lax.fori_loop(0, nblk, block, 0)


def _attn_call(kernel, grid, in_specs, out_specs, out_shape, lc, halo, name):
    return pl.pallas_call(
        kernel,
        grid=grid,
        in_specs=in_specs,
        out_specs=out_specs,
        out_shape=out_shape,
        scratch_shapes=[
            pltpu.VMEM((lc + 2 * halo, QUAD), _BF16),
            pltpu.VMEM((4, lc + 2 * halo, QUAD), _BF16),
        ],
        compiler_params=pltpu.CompilerParams(
            dimension_semantics=("arbitrary",) * len(grid), vmem_limit_bytes=VMEM_LIMIT_BYTES),
        name=name,
    )


def _chunk_rows(length):
    return min(length, 512)


def _attn_a(za, bias, sink, b, s):
    n = A_WINDOW
    lc = _chunk_rows(s)
    per_chunk = lc // n
    kcol, vcol = A_Q // A_KV, A_Q // A_KV + 1
    prev = lambda bi, c: (bi, jnp.maximum(c * per_chunk - 1, 0))
    nxt = lambda bi, c: (bi, jnp.minimum((c + 1) * per_chunk, s // n - 1))
    in_specs = [
        pl.BlockSpec((None, lc, A_Q), lambda bi, c: (bi, c, 0)),
        pl.BlockSpec((None, lc, A_KV), lambda bi, c: (bi, c, kcol)),
        pl.BlockSpec((None, lc, A_KV), lambda bi, c: (bi, c, vcol)),
        pl.BlockSpec((None, n, A_KV), lambda bi, c: prev(bi, c) + (kcol,)),
        pl.BlockSpec((None, n, A_KV), lambda bi, c: prev(bi, c) + (vcol,)),
        pl.BlockSpec((None, n, A_KV), lambda bi, c: nxt(bi, c) + (kcol,)),
        pl.BlockSpec((None, n, A_KV), lambda bi, c: nxt(bi, c) + (vcol,)),
        pl.BlockSpec(bias.shape, lambda bi, c: (0, 0, 0, 0), pipeline_mode=pl.Buffered(1)),
        pl.BlockSpec(memory_space=pltpu.SMEM),
    ]
    kernel = functools.partial(_attn_kernel, halo=n, lc=lc, nquad=2, krep=2, has_sink=True,
                               want_lse=False, chunk_axis=1)
    return _attn_call(
        kernel, (b, s // lc), in_specs,
        pl.BlockSpec((None, lc, A_Q), lambda bi, c: (bi, c, 0)),
        jax.ShapeDtypeStruct((b, s, A_Q), _BF16), lc, n, "attn_a",
    )(za, za, za, za, za, za, za, bias, sink)


def _attn_b(zb, bias, b, s, window, dil, name):
    n = window // (2 * dil)
    length = s // dil
    lc = _chunk_rows(length)
    per_chunk = lc // n
    zr = zb.reshape(b, length, dil * SLAB)
    prev = lambda bi, r, c: (bi, jnp.maximum(c * per_chunk - 1, 0))
    nxt = lambda bi, r, c: (bi, jnp.minimum((c + 1) * per_chunk, length // n - 1))
    in_specs = [
        pl.BlockSpec((None, lc, QUAD), lambda bi, r, c: (bi, c, 3 * r)),
        pl.BlockSpec((None, lc, QUAD), lambda bi, r, c: (bi, c, 3 * r + 1)),
        pl.BlockSpec((None, lc, QUAD), lambda bi, r, c: (bi, c, 3 * r + 2)),
        pl.BlockSpec((None, n, QUAD), lambda bi, r, c: prev(bi, r, c) + (3 * r + 1,)),
        pl.BlockSpec((None, n, QUAD), lambda bi, r, c: prev(bi, r, c) + (3 * r + 2,)),
        pl.BlockSpec((None, n, QUAD), lambda bi, r, c: nxt(bi, r, c) + (3 * r + 1,)),
        pl.BlockSpec((None, n, QUAD), lambda bi, r, c: nxt(bi, r, c) + (3 * r + 2,)),
        pl.BlockSpec(bias.shape, lambda bi, r, c: (0, 0, 0, 0), pipeline_mode=pl.Buffered(1)),
    ]
    out_spec = pl.BlockSpec((None, lc, QUAD), lambda bi, r, c: (bi, c, r))
    kernel = functools.partial(_attn_kernel, halo=n, lc=lc, nquad=1, krep=1, has_sink=False,
                               want_lse=True, chunk_axis=2)
    o, lse = _attn_call(
        kernel, (b, dil, length // lc), in_specs, [out_spec, out_spec],
        [jax.ShapeDtypeStruct((b, length, dil * QUAD), _BF16),
         jax.ShapeDtypeStruct((b, length, dil * QUAD), _F32)], lc, n, name,
    )(zr, zr, zr, zr, zr, zr, zr, bias)
    return o.reshape(b * s, QUAD), lse.reshape(b * s, QUAD)


def _rms(x, g):
    return x * lax.rsqrt(jnp.mean(x * x, axis=-1, keepdims=True) + RMS_EPS) * g


def _tail_kernel(x_ref, gate_ref, oa_ref, ob1_ref, ob2_ref, ob3_ref, l1_ref, l2_ref, l3_ref,
                 wa_ref, wb_ref, wo_ref, gm_ref, wu_ref, wd_ref, gf_ref, y_ref):
    lses = [l1_ref[...], l2_ref[...], l3_ref[...]]
    top = jnp.maximum(jnp.maximum(lses[0], lses[1]), lses[2])
    ws = [jnp.exp(l - top) for l in lses]
    mix = sum(w * o[...].astype(_F32) for w, o in zip(ws, (ob1_ref, ob2_ref, ob3_ref)))
    o_b = (mix / (ws[0] + ws[1] + ws[2])).astype(_BF16)

    pa = jnp.dot(oa_ref[...], wa_ref[...], preferred_element_type=_F32)
    pb = jnp.dot(o_b, wb_ref[...], preferred_element_type=_F32)
    merged = (gate_ref[:, :D_MODEL].astype(_F32) * pa
              + gate_ref[:, D_MODEL:].astype(_F32) * pb).astype(_BF16)
    x1 = x_ref[...] + jnp.dot(merged, wo_ref[...], preferred_element_type=_F32)

    h = _rms(x1, gm_ref[...]).astype(_BF16)
    acc = x1
    for c in range(D_FF // D_MODEL):
        cols = slice(c * D_MODEL, (c + 1) * D_MODEL)
        u = jnp.maximum(jnp.dot(h, wu_ref[:, cols], preferred_element_type=_F32), 0.0)
        acc = acc + jnp.dot((u * u).astype(_BF16), wd_ref[cols, :], preferred_element_type=_F32)
    y_ref[...] = _rms(acc, gf_ref[...])


def _tail(x2d, gates, o_a, o_bs, lses, wa, wb, wo, g_mlp, wu, wd, g_final, tm):
    t = x2d.shape[0]
    const = lambda i: (0, 0)
    row = lambda i: (i, 0)
    resident = lambda a: pl.BlockSpec(a.shape, const, pipeline_mode=pl.Buffered(1))
    in_specs = (
        [pl.BlockSpec((tm, D_MODEL), row), pl.BlockSpec((tm, 2 * D_MODEL), row),
         pl.BlockSpec((tm, A_Q), row)]
        + [pl.BlockSpec((tm, QUAD), row)] * 6
        + [resident(wa), resident(wb), resident(wo), resident(g_mlp), resident(wu), resident(wd),
           resident(g_final)]
    )
    return pl.pallas_call(
        _tail_kernel,
        grid=(t // tm,),
        in_specs=in_specs,
        out_specs=pl.BlockSpec((tm, D_MODEL), row),
        out_shape=jax.ShapeDtypeStruct((t, D_MODEL), _F32),
        compiler_params=pltpu.CompilerParams(
            dimension_semantics=("arbitrary",), vmem_limit_bytes=VMEM_LIMIT_BYTES),
        name="tail",
    )(x2d, gates, o_a, *o_bs, *lses, wa, wb, wo, g_mlp, wu, wd, g_final)


def _prepare(rel_bias, g_mix, w_in, b_gate, w_branch_a, w_branch_b, w_out, attn_sink, g_mlp,
             w_up, w_down, g_final):
    scale = HEAD_DIM ** -0.5
    order = np.asarray(A_HEAD_ORDER)
    o1, o2, o3 = A_Q, A_Q + A_KV, A_Q + 2 * A_KV
    o4, o5 = o3 + B_W, o3 + 2 * B_W
    o6 = o3 + 3 * B_W
    qa = w_in[:, :o1].reshape(D_MODEL, A_HEADS, HEAD_DIM)[:, order].reshape(D_MODEL, A_Q) * scale
    slabs = [qa, w_in[:, o1:o2], w_in[:, o2:o3]]
    for gi in range(len(B_GROUPS)):
        cols = slice(gi * B_OUT, (gi + 1) * B_OUT)
        slabs += [w_in[:, o3:o4][:, cols] * scale, w_in[:, o4:o5][:, cols], w_in[:, o5:o6][:, cols]]
    w_cat = jnp.concatenate(slabs + [w_in[:, o6:]], axis=1).astype(_BF16)

    wa = w_branch_a.reshape(A_HEADS, HEAD_DIM, D_MODEL)[order].reshape(A_Q, D_MODEL).astype(_BF16)
    bias_a = _bias_blocks(rel_bias[:, :A_HEADS][:, order], A_WINDOW, 1, Q_BLOCK + 2 * A_WINDOW)
    bias_b = []
    for gi, (window, dil) in enumerate(B_GROUPS):
        n = window // (2 * dil)
        table = rel_bias[:, A_HEADS + gi * B_HEADS_PER_GROUP:A_HEADS + (gi + 1) * B_HEADS_PER_GROUP]
        bias_b.append(_bias_blocks(table, n, dil, Q_BLOCK + 2 * n))
    return dict(
        w_cat=w_cat, g_mix=g_mix.reshape(1, D_MODEL), b_gate=b_gate.reshape(1, 2 * D_MODEL),
        wa=wa, wb=w_branch_b.astype(_BF16), wo=w_out.astype(_BF16),
        sink=attn_sink[order].astype(_F32), g_mlp=g_mlp.reshape(1, D_MODEL),
        wu=w_up.astype(_BF16), wd=w_down.astype(_BF16), g_final=g_final.reshape(1, D_MODEL),
        bias_a=bias_a, bias_b=bias_b,
    )


def _encode(x, p):
    b, s, _ = x.shape
    t = b * s
    x2d = x.reshape(t, D_MODEL)
    za, zb1, zb2, zb3, gates = _inproj(x2d, p["g_mix"], p["w_cat"], p["b_gate"], tm=512)
    o_a = _attn_a(za.reshape(b, s, SLAB), p["bias_a"], p["sink"], b, s).reshape(t, A_Q)
    o_bs, lses = [], []
    for gi, ((window, dil), zb) in enumerate(zip(B_GROUPS, (zb1, zb2, zb3))):
        o, lse = _attn_b(zb.reshape(b, s, SLAB), p["bias_b"][gi], b, s, window, dil, f"attn_b{gi}")
        o_bs.append(o)
        lses.append(lse)
    y = _tail(x2d, gates, o_a, o_bs, lses, p["wa"], p["wb"], p["wo"], p["g_mlp"], p["wu"], p["wd"],
              p["g_final"], tm=256)
    return y.reshape(b, s, D_MODEL)


def kernel(x_prompt, x_sample, rel_bias, g_mix, w_in, b_gate, w_branch_a, w_branch_b, w_out,
           attn_sink, g_mlp, w_up, w_down, g_final):
    assert g_mix.shape[0] == 1, "single-layer encoder"
    p = _prepare(rel_bias, g_mix[0], w_in[0], b_gate[0], w_branch_a[0], w_branch_b[0], w_out[0],
                 attn_sink[0], g_mlp[0], w_up[0], w_down[0], g_final)
    return (_encode(x_prompt, p), _encode(x_sample, p))
```

```python
import functools

import numpy as np
import jax
import jax.numpy as jnp
from jax import lax
from jax.experimental import pallas as pl
from jax.experimental.pallas import tpu as pltpu

D_MODEL = 1024
HEAD_DIM = 64
A_HEADS = 8
A_KV_HEADS = 2
A_WINDOW = 128
B_GROUPS = ((128, 1), (512, 4), (2048, 16))
B_HEADS_PER_GROUP = 4
B_HEADS = B_HEADS_PER_GROUP * len(B_GROUPS)
REL_BUCKETS = 32
REL_MAX_DISTANCE = 1024
D_FF = 4 * D_MODEL
A_Q = A_HEADS * HEAD_DIM
A_KV = A_KV_HEADS * HEAD_DIM
B_W = B_HEADS * HEAD_DIM
B_OUT = B_HEADS_PER_GROUP * HEAD_DIM
RMS_EPS = 1e-6
NEG_INF = -1e30

LANES = 128
QUAD = 4 * HEAD_DIM
SLAB = 3 * QUAD
Q_BLOCK = 128
VMEM_LIMIT_BYTES = 58 * 1024 * 1024

A_HEAD_ORDER = (0, 4, 1, 5, 2, 6, 3, 7)

_BF16 = jnp.bfloat16
_F32 = jnp.float32


def _rel_bucket(rel):
    half = REL_BUCKETS // 2
    max_exact = half // 2
    n = np.abs(rel)
    large = max_exact + (np.log(np.maximum(n, 1) / max_exact)
                         / np.log(REL_MAX_DISTANCE / max_exact) * (half - max_exact)).astype(np.int32)
    large = np.minimum(large, half - 1)
    return (np.where(rel > 0, half, 0) + np.where(n < max_exact, n, large)).astype(np.int32)


def _bias_blocks(table, n, dist_scale, nk):
    rel = np.arange(nk)[None, :] - n - np.arange(Q_BLOCK)[:, None]
    bucket = _rel_bucket(rel * dist_scale)
    table = table.astype(_F32)
    bias = jnp.zeros((table.shape[1], Q_BLOCK, nk), _F32)
    for bkt in np.unique(bucket):
        bias = jnp.where(jnp.asarray(bucket == bkt)[None], table[bkt][:, None, None], bias)
    band = np.abs(rel) <= n
    col = np.arange(nk)[None, :]
    variants = []
    for var in range(4):
        ok = band.copy()
        if var & 1:
            ok &= col >= n
        if var & 2:
            ok &= col < nk - n
        variants.append(jnp.where(ok[None], bias, NEG_INF))
    return jnp.stack(variants)


def _inproj_kernel(x_ref, g_ref, w_ref, b_ref, za_ref, zb1_ref, zb2_ref, zb3_ref, gate_ref, zs_ref,
                   *, tm):
    x = x_ref[...]
    ms = jnp.mean(x * x, axis=-1, keepdims=True)
    h = (x * lax.rsqrt(ms + RMS_EPS) * g_ref[...]).astype(_BF16)
    za_ref[...] = jnp.dot(h, w_ref[:, :SLAB], preferred_element_type=_F32).astype(_BF16)
    for gi, (out, (_, dil)) in enumerate(zip((zb1_ref, zb2_ref, zb3_ref), B_GROUPS)):
        z = jnp.dot(h, w_ref[:, (gi + 1) * SLAB:(gi + 2) * SLAB], preferred_element_type=_F32)
        if dil == 1:
            out[0] = z.astype(_BF16)
            continue
        for k in range(SLAB // LANES):
            zs_ref[k] = z[:, k * LANES:(k + 1) * LANES]
        for r in range(dil):
            for k in range(SLAB // LANES):
                out[r, :, k * LANES:(k + 1) * LANES] = (
                    zs_ref[k, pl.ds(r, tm // dil, stride=dil), :].astype(_BF16))
    zg = jnp.dot(h, w_ref[:, 4 * SLAB:], preferred_element_type=_F32) + b_ref[...]
    gate_ref[...] = jax.nn.sigmoid(zg).astype(_BF16)


def _inproj(x, g_mix, w_cat, b_gate, tm):
    b, s, _ = x.shape
    const = lambda bi, i: (0, 0)
    row = lambda bi, i: (bi, i, 0)
    res_specs = [pl.BlockSpec((None, dil, tm // dil, SLAB), lambda bi, i: (bi, 0, i, 0))
                 for _, dil in B_GROUPS]
    res_shapes = [jax.ShapeDtypeStruct((b, dil, s // dil, SLAB), _BF16) for _, dil in B_GROUPS]
    return pl.pallas_call(
        functools.partial(_inproj_kernel, tm=tm),
        grid=(b, s // tm),
        in_specs=[
            pl.BlockSpec((None, tm, D_MODEL), row),
            pl.BlockSpec((1, D_MODEL), const),
            pl.BlockSpec(w_cat.shape, const, pipeline_mode=pl.Buffered(1)),
            pl.BlockSpec((1, 2 * D_MODEL), const),
        ],
        out_specs=[pl.BlockSpec((None, tm, SLAB), row)] + res_specs
        + [pl.BlockSpec((None, tm, 2 * D_MODEL), row)],
        out_shape=[jax.ShapeDtypeStruct((b, s, SLAB), _BF16)] + res_shapes
        + [jax.ShapeDtypeStruct((b, s, 2 * D_MODEL), _BF16)],
        scratch_shapes=[pltpu.VMEM((SLAB // LANES, tm, LANES), _F32)],
        compiler_params=pltpu.CompilerParams(
            dimension_semantics=("arbitrary", "arbitrary"), vmem_limit_bytes=VMEM_LIMIT_BYTES),
        name="inproj",
    )(x, g_mix, w_cat, b_gate)


def _attn_kernel(*refs, halo, lc, nquad, krep, has_sink, want_lse, chunk_axis):
    it = iter(refs)
    q_ref, kc_ref, vc_ref, kp_ref, vp_ref, kn_ref, vn_ref, bias_ref = (next(it) for _ in range(8))
    sink_ref = next(it) if has_sink else None
    o_ref = next(it)
    lse_ref = next(it) if want_lse else None
    kext_ref, vm_ref = next(it), next(it)

    nk = Q_BLOCK + 2 * halo
    nblk = lc // Q_BLOCK
    c = pl.program_id(chunk_axis)
    last_c = pl.num_programs(chunk_axis) - 1

    def rep(v):
        return v if krep == 1 else jnp.concatenate([v] * krep, axis=-1)

    pieces = ((0, halo, kp_ref, vp_ref), (halo, lc, kc_ref, vc_ref), (halo + lc, halo, kn_ref, vn_ref))
    for lo, rows, k_src, v_src in pieces:
        kext_ref[lo:lo + rows, :] = rep(k_src[...])
        v = rep(v_src[...])
        grp = lax.broadcasted_iota(jnp.int32, v.shape, 1) // HEAD_DIM
        for j in range(4):
            vm_ref[j, lo:lo + rows, :] = jnp.where(grp == j, v, jnp.zeros_like(v))

    qgrp = lax.broadcasted_iota(jnp.int32, (Q_BLOCK, QUAD), 1) // HEAD_DIM

    def by_head(cols):
        out = jnp.broadcast_to(cols[3], (Q_BLOCK, QUAD))
        for j in (2, 1, 0):
            out = jnp.where(qgrp == j, cols[j], out)
        return out

    def block(i, carry):
        r0 = pl.multiple_of(i * Q_BLOCK, Q_BLOCK)
        var = (jnp.where((c == 0) & (i == 0), 1, 0)
               + jnp.where((c == last_c) & (i == nblk - 1), 2, 0))
        kk = kext_ref[pl.ds(r0, nk), :]
        vcat = jnp.concatenate([vm_ref[j, pl.ds(r0, nk), :] for j in range(4)], axis=0)
        for quad in range(nquad):
            q4 = q_ref[pl.ds(r0, Q_BLOCK), quad * QUAD:(quad + 1) * QUAD]
            qs = jnp.concatenate(
                [jnp.where(qgrp == j, q4, jnp.zeros_like(q4)) for j in range(4)], axis=0)
            s = lax.dot_general(qs, kk, (((1,), (1,)), ((), ())), preferred_element_type=_F32)
            ps, dens, maxes = [], [], []
            for j in range(4):
                head = quad * 4 + j
                logits = s[j * Q_BLOCK:(j + 1) * Q_BLOCK, :] + bias_ref[var, head]
                m = jnp.max(logits, axis=-1, keepdims=True)
                if has_sink:
                    sink = sink_ref[head]
                    m = jnp.maximum(m, sink)
                p = jnp.exp(logits - m)
                den = jnp.sum(p, axis=-1, keepdims=True)
                if has_sink:
                    den = den + jnp.exp(sink - m)
                ps.append(p.astype(_BF16))
                dens.append(den)
                maxes.append(m)
            o = jnp.dot(jnp.concatenate(ps, axis=-1), vcat, preferred_element_type=_F32)
            o_ref[pl.ds(r0, Q_BLOCK), quad * QUAD:(quad + 1) * QUAD] = (o / by_head(dens)).astype(_BF16)
            if want_lse:
                lse_ref[pl.ds(r0, Q_BLOCK), :] = by_head(
                    [m + jnp.log(d) for m, d in zip(maxes, dens)])
        return carry

    lax.fori_loop(0, nblk, block, 0, unroll=True)


def _attn_call(kernel, grid, in_specs, out_specs, out_shape, lc, halo, name):
    return pl.pallas_call(
        kernel,
        grid=grid,
        in_specs=in_specs,
        out_specs=out_specs,
        out_shape=out_shape,
        scratch_shapes=[
            pltpu.VMEM((lc + 2 * halo, QUAD), _BF16),
            pltpu.VMEM((4, lc + 2 * halo, QUAD), _BF16),
        ],
        compiler_params=pltpu.CompilerParams(
            dimension_semantics=("arbitrary",) * len(grid), vmem_limit_bytes=VMEM_LIMIT_BYTES),
        name=name,
    )


def _chunk_rows(length):
    return min(length, 512)


def _attn_a(za, bias, sink):
    b, s, _ = za.shape
    n = A_WINDOW
    lc = _chunk_rows(s)
    per_chunk = lc // n
    kcol, vcol = A_Q // A_KV, A_Q // A_KV + 1
    prev = lambda bi, c: (bi, jnp.maximum(c * per_chunk - 1, 0))
    nxt = lambda bi, c: (bi, jnp.minimum((c + 1) * per_chunk, s // n - 1))
    in_specs = [
        pl.BlockSpec((None, lc, A_Q), lambda bi, c: (bi, c, 0)),
        pl.BlockSpec((None, lc, A_KV), lambda bi, c: (bi, c, kcol)),
        pl.BlockSpec((None, lc, A_KV), lambda bi, c: (bi, c, vcol)),
        pl.BlockSpec((None, n, A_KV), lambda bi, c: prev(bi, c) + (kcol,)),
        pl.BlockSpec((None, n, A_KV), lambda bi, c: prev(bi, c) + (vcol,)),
        pl.BlockSpec((None, n, A_KV), lambda bi, c: nxt(bi, c) + (kcol,)),
        pl.BlockSpec((None, n, A_KV), lambda bi, c: nxt(bi, c) + (vcol,)),
        pl.BlockSpec(bias.shape, lambda bi, c: (0, 0, 0, 0), pipeline_mode=pl.Buffered(1)),
        pl.BlockSpec(memory_space=pltpu.SMEM),
    ]
    kernel = functools.partial(_attn_kernel, halo=n, lc=lc, nquad=2, krep=2, has_sink=True,
                               want_lse=False, chunk_axis=1)
    return _attn_call(
        kernel, (b, s // lc), in_specs,
        pl.BlockSpec((None, lc, A_Q), lambda bi, c: (bi, c, 0)),
        jax.ShapeDtypeStruct((b, s, A_Q), _BF16), lc, n, "attn_a",
    )(za, za, za, za, za, za, za, bias, sink)


def _attn_b(zr, bias, window, name):
    b, dil, length, _ = zr.shape
    n = window // (2 * dil)
    lc = _chunk_rows(length)
    per_chunk = lc // n
    prev = lambda bi, r, c: (bi, r, jnp.maximum(c * per_chunk - 1, 0))
    nxt = lambda bi, r, c: (bi, r, jnp.minimum((c + 1) * per_chunk, length // n - 1))
    in_specs = [
        pl.BlockSpec((None, None, lc, QUAD), lambda bi, r, c: (bi, r, c, 0)),
        pl.BlockSpec((None, None, lc, QUAD), lambda bi, r, c: (bi, r, c, 1)),
        pl.BlockSpec((None, None, lc, QUAD), lambda bi, r, c: (bi, r, c, 2)),
        pl.BlockSpec((None, None, n, QUAD), lambda bi, r, c: prev(bi, r, c) + (1,)),
        pl.BlockSpec((None, None, n, QUAD), lambda bi, r, c: prev(bi, r, c) + (2,)),
        pl.BlockSpec((None, None, n, QUAD), lambda bi, r, c: nxt(bi, r, c) + (1,)),
        pl.BlockSpec((None, None, n, QUAD), lambda bi, r, c: nxt(bi, r, c) + (2,)),
        pl.BlockSpec(bias.shape, lambda bi, r, c: (0, 0, 0, 0), pipeline_mode=pl.Buffered(1)),
    ]
    out_spec = pl.BlockSpec((None, None, lc, QUAD), lambda bi, r, c: (bi, r, c, 0))
    kernel = functools.partial(_attn_kernel, halo=n, lc=lc, nquad=1, krep=1, has_sink=False,
                               want_lse=True, chunk_axis=2)
    return _attn_call(
        kernel, (b, dil, length // lc), in_specs, [out_spec, out_spec],
        [jax.ShapeDtypeStruct((b, dil, length, QUAD), _BF16),
         jax.ShapeDtypeStruct((b, dil, length, QUAD), _F32)], lc, n, name,
    )(zr, zr, zr, zr, zr, zr, zr, bias)


def _rms(x, g):
    return x * lax.rsqrt(jnp.mean(x * x, axis=-1, keepdims=True) + RMS_EPS) * g


def _tokens(src_ref, stage_ref, tm):
    dil = src_ref.shape[0]
    if dil == 1:
        return src_ref[0].astype(_F32)
    for r in range(dil):
        for k in range(QUAD // LANES):
            stage_ref[k, pl.ds(r, tm // dil, stride=dil), :] = (
                src_ref[r, :, k * LANES:(k + 1) * LANES].astype(_F32))
    return jnp.concatenate([stage_ref[k] for k in range(QUAD // LANES)], axis=-1)


def _tail_kernel(x_ref, gate_ref, oa_ref, ob1_ref, ob2_ref, ob3_ref, l1_ref, l2_ref, l3_ref,
                 wa_ref, wb_ref, wo_ref, gm_ref, wu_ref, wd_ref, gf_ref, y_ref,
                 so2_ref, so3_ref, sl2_ref, sl3_ref, *, tm):
    outs = [_tokens(ob1_ref, None, tm), _tokens(ob2_ref, so2_ref, tm), _tokens(ob3_ref, so3_ref, tm)]
    lses = [_tokens(l1_ref, None, tm), _tokens(l2_ref, sl2_ref, tm), _tokens(l3_ref, sl3_ref, tm)]
    top = jnp.maximum(jnp.maximum(lses[0], lses[1]), lses[2])
    ws = [jnp.exp(l - top) for l in lses]
    mix = ws[0] * outs[0] + ws[1] * outs[1] + ws[2] * outs[2]
    o_b = (mix / (ws[0] + ws[1] + ws[2])).astype(_BF16)

    pa = jnp.dot(oa_ref[...], wa_ref[...], preferred_element_type=_F32)
    pb = jnp.dot(o_b, wb_ref[...], preferred_element_type=_F32)
    merged = (gate_ref[:, :D_MODEL].astype(_F32) * pa
              + gate_ref[:, D_MODEL:].astype(_F32) * pb).astype(_BF16)
    x1 = x_ref[...] + jnp.dot(merged, wo_ref[...], preferred_element_type=_F32)

    h = _rms(x1, gm_ref[...]).astype(_BF16)
    acc = x1
    for c in range(D_FF // D_MODEL):
        cols = slice(c * D_MODEL, (c + 1) * D_MODEL)
        u = jnp.maximum(jnp.dot(h, wu_ref[:, cols], preferred_element_type=_F32), 0.0)
        acc = acc + jnp.dot((u * u).astype(_BF16), wd_ref[cols, :], preferred_element_type=_F32)
    y_ref[...] = _rms(acc, gf_ref[...])


def _tail(x, gates, o_a, o_bs, lses, wa, wb, wo, g_mlp, wu, wd, g_final, tm):
    b, s, _ = x.shape
    const = lambda bi, i: (0, 0)
    row = lambda bi, i: (bi, i, 0)
    resident = lambda a: pl.BlockSpec(a.shape, const, pipeline_mode=pl.Buffered(1))
    res_specs = [pl.BlockSpec((None, dil, tm // dil, QUAD), lambda bi, i: (bi, 0, i, 0))
                 for _, dil in B_GROUPS]
    in_specs = (
        [pl.BlockSpec((None, tm, D_MODEL), row), pl.BlockSpec((None, tm, 2 * D_MODEL), row),
         pl.BlockSpec((None, tm, A_Q), row)]
        + res_specs + res_specs
        + [resident(wa), resident(wb), resident(wo), resident(g_mlp), resident(wu), resident(wd),
           resident(g_final)]
    )
    stage = pltpu.VMEM((QUAD // LANES, tm, LANES), _F32)
    return pl.pallas_call(
        functools.partial(_tail_kernel, tm=tm),
        grid=(b, s // tm),
        in_specs=in_specs,
        out_specs=pl.BlockSpec((None, tm, D_MODEL), row),
        out_shape=jax.ShapeDtypeStruct((b, s, D_MODEL), _F32),
        scratch_shapes=[stage] * 4,
        compiler_params=pltpu.CompilerParams(
            dimension_semantics=("arbitrary", "arbitrary"), vmem_limit_bytes=VMEM_LIMIT_BYTES),
        name="tail",
    )(x, gates, o_a, *o_bs, *lses, wa, wb, wo, g_mlp, wu, wd, g_final)


def _prepare(rel_bias, g_mix, w_in, b_gate, w_branch_a, w_branch_b, w_out, attn_sink, g_mlp,
             w_up, w_down, g_final):
    scale = HEAD_DIM ** -0.5
    order = np.asarray(A_HEAD_ORDER)
    o1, o2, o3 = A_Q, A_Q + A_KV, A_Q + 2 * A_KV
    o4, o5 = o3 + B_W, o3 + 2 * B_W
    o6 = o3 + 3 * B_W
    qa = w_in[:, :o1].reshape(D_MODEL, A_HEADS, HEAD_DIM)[:, order].reshape(D_MODEL, A_Q) * scale
    slabs = [qa, w_in[:, o1:o2], w_in[:, o2:o3]]
    for gi in range(len(B_GROUPS)):
        cols = slice(gi * B_OUT, (gi + 1) * B_OUT)
        slabs += [w_in[:, o3:o4][:, cols] * scale, w_in[:, o4:o5][:, cols], w_in[:, o5:o6][:, cols]]
    w_cat = jnp.concatenate(slabs + [w_in[:, o6:]], axis=1).astype(_BF16)

    wa = w_branch_a.reshape(A_HEADS, HEAD_DIM, D_MODEL)[order].reshape(A_Q, D_MODEL).astype(_BF16)
    bias_a = _bias_blocks(rel_bias[:, :A_HEADS][:, order], A_WINDOW, 1, Q_BLOCK + 2 * A_WINDOW)
    bias_b = []
    for gi, (window, dil) in enumerate(B_GROUPS):
        n = window // (2 * dil)
        table = rel_bias[:, A_HEADS + gi * B_HEADS_PER_GROUP:A_HEADS + (gi + 1) * B_HEADS_PER_GROUP]
        bias_b.append(_bias_blocks(table, n, dil, Q_BLOCK + 2 * n))
    return dict(
        w_cat=w_cat, g_mix=g_mix.reshape(1, D_MODEL), b_gate=b_gate.reshape(1, 2 * D_MODEL),
        wa=wa, wb=w_branch_b.astype(_BF16), wo=w_out.astype(_BF16),
        sink=attn_sink[order].astype(_F32), g_mlp=g_mlp.reshape(1, D_MODEL),
        wu=w_up.astype(_BF16), wd=w_down.astype(_BF16), g_final=g_final.reshape(1, D_MODEL),
        bias_a=bias_a, bias_b=bias_b,
    )


def _encode(x, p):
    za, zb1, zb2, zb3, gates = _inproj(x, p["g_mix"], p["w_cat"], p["b_gate"], tm=512)
    o_a = _attn_a(za, p["bias_a"], p["sink"])
    o_bs, lses = [], []
    for gi, ((window, _), zr) in enumerate(zip(B_GROUPS, (zb1, zb2, zb3))):
        o, lse = _attn_b(zr, p["bias_b"][gi], window, f"attn_b{gi}")
        o_bs.append(o)
        lses.append(lse)
    return _tail(x, gates, o_a, o_bs, lses, p["wa"], p["wb"], p["wo"], p["g_mlp"], p["wu"], p["wd"],
                 p["g_final"], tm=256)


def kernel(x_prompt, x_sample, rel_bias, g_mix, w_in, b_gate, w_branch_a, w_branch_b, w_out,
           attn_sink, g_mlp, w_up, w_down, g_final):
    assert g_mix.shape[0] == 1, "single-layer encoder"
    p = _prepare(rel_bias, g_mix[0], w_in[0], b_gate[0], w_branch_a[0], w_branch_b[0], w_out[0],
                 attn_sink[0], g_mlp[0], w_up[0], w_down[0], g_final)
    return (_encode(x_prompt, p), _encode(x_sample, p))
```

```python
import functools
import math

import numpy as np
import jax
import jax.numpy as jnp
from jax import lax
from jax.experimental import pallas as pl
from jax.experimental.pallas import tpu as pltpu

D_MODEL = 1024
HEAD_DIM = 64
A_HEADS = 8
A_KV_HEADS = 2
A_WINDOW = 128
B_GROUPS = ((128, 1), (512, 4), (2048, 16))
B_HEADS_PER_GROUP = 4
B_HEADS = B_HEADS_PER_GROUP * len(B_GROUPS)
REL_BUCKETS = 32
REL_MAX_DISTANCE = 1024
D_FF = 4 * D_MODEL
A_Q = A_HEADS * HEAD_DIM
A_KV = A_KV_HEADS * HEAD_DIM
B_W = B_HEADS * HEAD_DIM
B_OUT = B_HEADS_PER_GROUP * HEAD_DIM
RMS_EPS = 1e-6
NEG_INF = -1e30
LOG2E = math.log2(math.e)
LN2 = math.log(2.0)

LANES = 128
QUAD = 4 * HEAD_DIM
PAIR = 2 * HEAD_DIM
SLAB = 3 * QUAD
Q_BLOCK = 128
VMEM_LIMIT_BYTES = 58 * 1024 * 1024

A_HEAD_ORDER = (0, 4, 1, 5, 2, 6, 3, 7)

_BF16 = jnp.bfloat16
_F32 = jnp.float32


def _rel_bucket(rel):
    half = REL_BUCKETS // 2
    max_exact = half // 2
    n = np.abs(rel)
    large = max_exact + (np.log(np.maximum(n, 1) / max_exact)
                         / np.log(REL_MAX_DISTANCE / max_exact) * (half - max_exact)).astype(np.int32)
    large = np.minimum(large, half - 1)
    return (np.where(rel > 0, half, 0) + np.where(n < max_exact, n, large)).astype(np.int32)


def _bias_blocks(table, n, dist_scale, nk):
    rel = np.arange(nk)[:, None] - n - np.arange(Q_BLOCK)[None, :]
    bucket = _rel_bucket(rel * dist_scale)
    table = table.astype(_F32) * LOG2E
    bias = jnp.zeros((table.shape[1], nk, Q_BLOCK), _F32)
    for bkt in np.unique(bucket):
        bias = jnp.where(jnp.asarray(bucket == bkt)[None], table[bkt][:, None, None], bias)
    band = np.abs(rel) <= n
    key = np.arange(nk)[:, None]
    variants = []
    for var in range(4):
        ok = band.copy()
        if var & 1:
            ok &= key >= n
        if var & 2:
            ok &= key < nk - n
        variants.append(jnp.where(ok[None], bias, NEG_INF))
    return jnp.stack(variants)


def _inproj_kernel(x_ref, g_ref, w_ref, b_ref, za_ref, zb1_ref, zb2_ref, zb3_ref, gate_ref, zs_ref,
                   *, tm):
    x = x_ref[...]
    ms = jnp.mean(x * x, axis=-1, keepdims=True)
    h = (x * lax.rsqrt(ms + RMS_EPS) * g_ref[...]).astype(_BF16)
    zg = jnp.dot(h, w_ref[:, 4 * SLAB:], preferred_element_type=_F32) + b_ref[...]
    gate_ref[...] = (0.5 * jnp.tanh(0.5 * zg) + 0.5).astype(_BF16)
    za_ref[...] = jnp.dot(h, w_ref[:, :SLAB], preferred_element_type=_F32).astype(_BF16)
    for gi, (out, (_, dil)) in enumerate(zip((zb1_ref, zb2_ref, zb3_ref), B_GROUPS)):
        z = jnp.dot(h, w_ref[:, (gi + 1) * SLAB:(gi + 2) * SLAB], preferred_element_type=_F32)
        if dil == 1:
            out[0] = z.astype(_BF16)
            continue
        for k in range(SLAB // LANES):
            zs_ref[k] = z[:, k * LANES:(k + 1) * LANES]
        for r in range(dil):
            for k in range(SLAB // LANES):
                out[r, :, k * LANES:(k + 1) * LANES] = (
                    zs_ref[k, pl.ds(r, tm // dil, stride=dil), :].astype(_BF16))


def _inproj(x, g_mix, w_cat, b_gate, tm):
    b, s, _ = x.shape
    const = lambda bi, i: (0, 0)
    row = lambda bi, i: (bi, i, 0)
    res_specs = [pl.BlockSpec((None, dil, tm // dil, SLAB), lambda bi, i: (bi, 0, i, 0))
                 for _, dil in B_GROUPS]
    res_shapes = [jax.ShapeDtypeStruct((b, dil, s // dil, SLAB), _BF16) for _, dil in B_GROUPS]
    return pl.pallas_call(
        functools.partial(_inproj_kernel, tm=tm),
        grid=(b, s // tm),
        in_specs=[
            pl.BlockSpec((None, tm, D_MODEL), row),
            pl.BlockSpec((1, D_MODEL), const),
            pl.BlockSpec(w_cat.shape, const, pipeline_mode=pl.Buffered(1)),
            pl.BlockSpec((1, 2 * D_MODEL), const),
        ],
        out_specs=[pl.BlockSpec((None, tm, SLAB), row)] + res_specs
        + [pl.BlockSpec((None, tm, 2 * D_MODEL), row)],
        out_shape=[jax.ShapeDtypeStruct((b, s, SLAB), _BF16)] + res_shapes
        + [jax.ShapeDtypeStruct((b, s, 2 * D_MODEL), _BF16)],
        scratch_shapes=[pltpu.VMEM((SLAB // LANES, tm, LANES), _F32)],
        compiler_params=pltpu.CompilerParams(
            dimension_semantics=("arbitrary", "arbitrary"), vmem_limit_bytes=VMEM_LIMIT_BYTES),
        name="inproj",
    )(x, g_mix, w_cat, b_gate)


def _attn_kernel(*refs, halo, lc, nres, nquad, kv_pairs, has_sink, want_lse, chunk_axis):
    it = iter(refs)
    q_ref, kc_ref, vc_ref, kp_ref, vp_ref, kn_ref, vn_ref, bias_ref = (next(it) for _ in range(8))
    sink_ref = next(it) if has_sink else None
    o_ref = next(it)
    lse_ref = next(it) if want_lse else None
    kext_ref, vext_ref, vt_ref = next(it), next(it), next(it)

    nk = Q_BLOCK + 2 * halo
    nblk = lc // Q_BLOCK
    c = pl.program_id(chunk_axis)
    first = jnp.where(c == 0, 1, 0)
    last = jnp.where(c == pl.num_programs(chunk_axis) - 1, 2, 0)

    qgrp = lax.broadcasted_iota(jnp.int32, (Q_BLOCK, QUAD), 1) // HEAD_DIM

    def stage(r):
        pieces = ((0, halo, kp_ref, vp_ref), (halo, lc, kc_ref, vc_ref),
                  (halo + lc, halo, kn_ref, vn_ref))
        for lo, rows, k_src, v_src in pieces:
            k = k_src[r]
            kext_ref[r, lo:lo + rows, :] = k if kv_pairs == 2 else jnp.concatenate([k, k], axis=-1)
            vext_ref[r, lo:lo + rows, :] = v_src[r]
        vt_ref[r] = vext_ref[r].astype(_F32).T.astype(_BF16)

    def scores(r, i, quad):
        if i == 0 and quad == 0:
            stage(r)
        r0 = i * Q_BLOCK
        q4 = q_ref[r, r0:r0 + Q_BLOCK, quad * QUAD:(quad + 1) * QUAD]
        qs = jnp.concatenate(
            [jnp.where(qgrp == j, q4, jnp.zeros_like(q4)) for j in range(4)], axis=0)
        return lax.dot_general(kext_ref[r, r0:r0 + nk, :], qs, (((1,), (1,)), ((), ())),
                               preferred_element_type=_F32)

    def finish(r, i, quad, st):
        r0 = i * Q_BLOCK
        var = (first if i == 0 else 0) + (last if i == nblk - 1 else 0)
        pts, scales, lses = [], [], []
        for j in range(4):
            head = quad * 4 + j
            logits = st[:, j * Q_BLOCK:(j + 1) * Q_BLOCK] + bias_ref[var, head]
            m = jnp.max(logits, axis=0, keepdims=True)
            if has_sink:
                sink = sink_ref[head]
                m = jnp.maximum(m, sink)
            p = jnp.exp2(logits - m)
            den = jnp.sum(p, axis=0, keepdims=True)
            if has_sink:
                den = den + jnp.exp2(sink - m)
            pts.append(p.astype(_BF16))
            scales.append(1.0 / den)
            lses.append(m * LN2 + jnp.log(den))
        outs = []
        for pair in range(2):
            j0 = 2 * pair
            vrows = (pair % kv_pairs) * PAIR
            ot = jnp.dot(vt_ref[r, vrows:vrows + PAIR, r0:r0 + nk],
                         jnp.concatenate(pts[j0:j0 + 2], axis=1), preferred_element_type=_F32)
            outs.append(ot[:HEAD_DIM, :Q_BLOCK] * scales[j0])
            outs.append(ot[HEAD_DIM:, Q_BLOCK:] * scales[j0 + 1])
        o_ref[r, r0:r0 + Q_BLOCK, quad * QUAD:(quad + 1) * QUAD] = (
            jnp.concatenate(outs, axis=0).T.astype(_BF16))
        if want_lse:
            lse_t = jnp.concatenate(
                [jnp.broadcast_to(l, (HEAD_DIM, Q_BLOCK)) for l in lses], axis=0)
            lse_ref[r, r0:r0 + Q_BLOCK, :] = lse_t.T

    units = [(r, i, quad) for r in range(nres) for i in range(nblk) for quad in range(nquad)]
    ahead = 2 if nquad == 1 else 1
    pending = [scores(*u) for u in units[:ahead]]
    for idx, unit in enumerate(units):
        if idx + ahead < len(units):
            pending.append(scores(*units[idx + ahead]))
        finish(*unit, pending.pop(0))


UNITS_PER_STEP = 16


def _attn_call(zr, bias, sink, *, halo, lc, nres, qw, kvw, kcol, vcol, want_lse, name):
    b, dil, length, _ = zr.shape
    per_chunk = lc // halo
    ext = lc + 2 * halo
    prev = lambda bi, r, c: (bi, r, jnp.maximum(c * per_chunk - 1, 0))
    nxt = lambda bi, r, c: (bi, r, jnp.minimum((c + 1) * per_chunk, length // halo - 1))
    in_specs = [
        pl.BlockSpec((None, nres, lc, qw), lambda bi, r, c: (bi, r, c, 0)),
        pl.BlockSpec((None, nres, lc, kvw), lambda bi, r, c: (bi, r, c, kcol)),
        pl.BlockSpec((None, nres, lc, kvw), lambda bi, r, c: (bi, r, c, vcol)),
        pl.BlockSpec((None, nres, halo, kvw), lambda bi, r, c: prev(bi, r, c) + (kcol,)),
        pl.BlockSpec((None, nres, halo, kvw), lambda bi, r, c: prev(bi, r, c) + (vcol,)),
        pl.BlockSpec((None, nres, halo, kvw), lambda bi, r, c: nxt(bi, r, c) + (kcol,)),
        pl.BlockSpec((None, nres, halo, kvw), lambda bi, r, c: nxt(bi, r, c) + (vcol,)),
        pl.BlockSpec(bias.shape, lambda bi, r, c: (0, 0, 0, 0), pipeline_mode=pl.Buffered(1)),
    ]
    args = [zr] * 7 + [bias]
    if sink is not None:
        in_specs.append(pl.BlockSpec(memory_space=pltpu.SMEM))
        args.append(sink)
    out_spec = pl.BlockSpec((None, nres, lc, qw), lambda bi, r, c: (bi, r, c, 0))
    out_specs, out_shape = [out_spec], [jax.ShapeDtypeStruct((b, dil, length, qw), _BF16)]
    if want_lse:
        out_specs.append(out_spec)
        out_shape.append(jax.ShapeDtypeStruct((b, dil, length, qw), _F32))
    kernel = functools.partial(
        _attn_kernel, halo=halo, lc=lc, nres=nres, nquad=qw // QUAD, kv_pairs=kvw // PAIR,
        has_sink=sink is not None, want_lse=want_lse, chunk_axis=2)
    grid = (b, dil // nres, length // lc)
    return pl.pallas_call(
        kernel,
        grid=grid,
        in_specs=in_specs,
        out_specs=out_specs,
        out_shape=out_shape,
        scratch_shapes=[
            pltpu.VMEM((nres, ext, QUAD), _BF16),
            pltpu.VMEM((nres, ext, kvw), _BF16),
            pltpu.VMEM((nres, kvw, ext), _BF16),
        ],
        compiler_params=pltpu.CompilerParams(
            dimension_semantics=("arbitrary",) * len(grid), vmem_limit_bytes=VMEM_LIMIT_BYTES),
        name=name,
    )(*args)


def _attn_a(za, bias, sink):
    b, s, _ = za.shape
    lc = min(s, Q_BLOCK * UNITS_PER_STEP // (A_Q // QUAD))
    (o,) = _attn_call(za.reshape(b, 1, s, SLAB), bias, sink, halo=A_WINDOW, lc=lc, nres=1, qw=A_Q,
                      kvw=A_KV, kcol=A_Q // A_KV, vcol=A_Q // A_KV + 1, want_lse=False,
                      name="attn_a")
    return o.reshape(b, s, A_Q)


def _attn_b(zr, bias, window, name):
    _, dil, length, _ = zr.shape
    lc = min(length, 512 if dil > 1 else Q_BLOCK * UNITS_PER_STEP)
    nres = min(dil, UNITS_PER_STEP // (lc // Q_BLOCK))
    return _attn_call(zr, bias, None, halo=window // (2 * dil), lc=lc, nres=nres, qw=QUAD,
                      kvw=QUAD, kcol=1, vcol=2, want_lse=True, name=name)


def _rms(x, g):
    return x * lax.rsqrt(jnp.mean(x * x, axis=-1, keepdims=True) + RMS_EPS) * g


def _tokens(src_ref, stage_ref, tm):
    dil = src_ref.shape[0]
    if dil == 1:
        return src_ref[0].astype(_F32)
    for r in range(dil):
        for k in range(QUAD // LANES):
            stage_ref[k, pl.ds(r, tm // dil, stride=dil), :] = (
                src_ref[r, :, k * LANES:(k + 1) * LANES].astype(_F32))
    return jnp.concatenate([stage_ref[k] for k in range(QUAD // LANES)], axis=-1)


def _tail_kernel(x_ref, gate_ref, oa_ref, ob1_ref, ob2_ref, ob3_ref, l1_ref, l2_ref, l3_ref,
                 wa_ref, wb_ref, wo_ref, gm_ref, wu_ref, wd_ref, gf_ref, y_ref,
                 so2_ref, so3_ref, sl2_ref, sl3_ref, *, tm):
    outs = [_tokens(ob1_ref, None, tm), _tokens(ob2_ref, so2_ref, tm), _tokens(ob3_ref, so3_ref, tm)]
    lses = [_tokens(l1_ref, None, tm), _tokens(l2_ref, sl2_ref, tm), _tokens(l3_ref, sl3_ref, tm)]
    top = jnp.maximum(jnp.maximum(lses[0], lses[1]), lses[2])
    ws = [jnp.exp(l - top) for l in lses]
    mix = ws[0] * outs[0] + ws[1] * outs[1] + ws[2] * outs[2]
    o_b = (mix / (ws[0] + ws[1] + ws[2])).astype(_BF16)

    pa = jnp.dot(oa_ref[...], wa_ref[...], preferred_element_type=_F32)
    pb = jnp.dot(o_b, wb_ref[...], preferred_element_type=_F32)
    merged = (gate_ref[:, :D_MODEL].astype(_F32) * pa
              + gate_ref[:, D_MODEL:].astype(_F32) * pb).astype(_BF16)
    x1 = x_ref[...] + jnp.dot(merged, wo_ref[...], preferred_element_type=_F32)

    h = _rms(x1, gm_ref[...]).astype(_BF16)
    acc = x1
    for c in range(D_FF // D_MODEL):
        cols = slice(c * D_MODEL, (c + 1) * D_MODEL)
        u = jnp.maximum(jnp.dot(h, wu_ref[:, cols], preferred_element_type=_F32), 0.0)
        acc = acc + jnp.dot((u * u).astype(_BF16), wd_ref[cols, :], preferred_element_type=_F32)
    y_ref[...] = _rms(acc, gf_ref[...])


def _tail(x, gates, o_a, o_bs, lses, wa, wb, wo, g_mlp, wu, wd, g_final, tm):
    b, s, _ = x.shape
    const = lambda bi, i: (0, 0)
    row = lambda bi, i: (bi, i, 0)
    resident = lambda a: pl.BlockSpec(a.shape, const, pipeline_mode=pl.Buffered(1))
    res_specs = [pl.BlockSpec((None, dil, tm // dil, QUAD), lambda bi, i: (bi, 0, i, 0))
                 for _, dil in B_GROUPS]
    in_specs = (
        [pl.BlockSpec((None, tm, D_MODEL), row), pl.BlockSpec((None, tm, 2 * D_MODEL), row),
         pl.BlockSpec((None, tm, A_Q), row)]
        + res_specs + res_specs
        + [resident(wa), resident(wb), resident(wo), resident(g_mlp), resident(wu), resident(wd),
           resident(g_final)]
    )
    stage = pltpu.VMEM((QUAD // LANES, tm, LANES), _F32)
    return pl.pallas_call(
        functools.partial(_tail_kernel, tm=tm),
        grid=(b, s // tm),
        in_specs=in_specs,
        out_specs=pl.BlockSpec((None, tm, D_MODEL), row),
        out_shape=jax.ShapeDtypeStruct((b, s, D_MODEL), _F32),
        scratch_shapes=[stage] * 4,
        compiler_params=pltpu.CompilerParams(
            dimension_semantics=("arbitrary", "arbitrary"), vmem_limit_bytes=VMEM_LIMIT_BYTES),
        name="tail",
    )(x, gates, o_a, *o_bs, *lses, wa, wb, wo, g_mlp, wu, wd, g_final)


def _prepare(rel_bias, g_mix, w_in, b_gate, w_branch_a, w_branch_b, w_out, attn_sink, g_mlp,
             w_up, w_down, g_final):
    scale = HEAD_DIM ** -0.5 * LOG2E
    order = np.asarray(A_HEAD_ORDER)
    o1, o2, o3 = A_Q, A_Q + A_KV, A_Q + 2 * A_KV
    o4, o5 = o3 + B_W, o3 + 2 * B_W
    o6 = o3 + 3 * B_W
    qa = w_in[:, :o1].reshape(D_MODEL, A_HEADS, HEAD_DIM)[:, order].reshape(D_MODEL, A_Q) * scale
    slabs = [qa, w_in[:, o1:o2], w_in[:, o2:o3]]
    for gi in range(len(B_GROUPS)):
        cols = slice(gi * B_OUT, (gi + 1) * B_OUT)
        slabs += [w_in[:, o3:o4][:, cols] * scale, w_in[:, o4:o5][:, cols], w_in[:, o5:o6][:, cols]]
    w_cat = jnp.concatenate(slabs + [w_in[:, o6:]], axis=1).astype(_BF16)

    wa = w_branch_a.reshape(A_HEADS, HEAD_DIM, D_MODEL)[order].reshape(A_Q, D_MODEL).astype(_BF16)
    bias_a = _bias_blocks(rel_bias[:, :A_HEADS][:, order], A_WINDOW, 1, Q_BLOCK + 2 * A_WINDOW)
    bias_b = []
    for gi, (window, dil) in enumerate(B_GROUPS):
        n = window // (2 * dil)
        table = rel_bias[:, A_HEADS + gi * B_HEADS_PER_GROUP:A_HEADS + (gi + 1) * B_HEADS_PER_GROUP]
        bias_b.append(_bias_blocks(table, n, dil, Q_BLOCK + 2 * n))
    return dict(
        w_cat=w_cat, g_mix=g_mix.reshape(1, D_MODEL), b_gate=b_gate.reshape(1, 2 * D_MODEL),
        wa=wa, wb=w_branch_b.astype(_BF16), wo=w_out.astype(_BF16),
        sink=attn_sink[order].astype(_F32) * LOG2E, g_mlp=g_mlp.reshape(1, D_MODEL),
        wu=w_up.astype(_BF16), wd=w_down.astype(_BF16), g_final=g_final.reshape(1, D_MODEL),
        bias_a=bias_a, bias_b=bias_b,
    )


def _encode(x, p):
    za, zb1, zb2, zb3, gates = _inproj(x, p["g_mix"], p["w_cat"], p["b_gate"], tm=512)
    o_a = _attn_a(za, p["bias_a"], p["sink"])
    o_bs, lses = [], []
    for gi, ((window, _), zr) in enumerate(zip(B_GROUPS, (zb1, zb2, zb3))):
        o, lse = _attn_b(zr, p["bias_b"][gi], window, f"attn_b{gi}")
        o_bs.append(o)
        lses.append(lse)
    return _tail(x, gates, o_a, o_bs, lses, p["wa"], p["wb"], p["wo"], p["g_mlp"], p["wu"], p["wd"],
                 p["g_final"], tm=512)


def kernel(x_prompt, x_sample, rel_bias, g_mix, w_in, b_gate, w_branch_a, w_branch_b, w_out,
           attn_sink, g_mlp, w_up, w_down, g_final):
    assert g_mix.shape[0] == 1, "single-layer encoder"
    p = _prepare(rel_bias, g_mix[0], w_in[0], b_gate[0], w_branch_a[0], w_branch_b[0], w_out[0],
                 attn_sink[0], g_mlp[0], w_up[0], w_down[0], g_final)
    return (_encode(x_prompt, p), _encode(x_sample, p))
```

```python
import functools
import math

import numpy as np
import jax
import jax.numpy as jnp
from jax import lax
from jax.experimental import pallas as pl
from jax.experimental.pallas import tpu as pltpu

D_MODEL = 1024
HEAD_DIM = 64
A_HEADS = 8
A_KV_HEADS = 2
A_WINDOW = 128
B_GROUPS = ((128, 1), (512, 4), (2048, 16))
B_HEADS_PER_GROUP = 4
B_HEADS = B_HEADS_PER_GROUP * len(B_GROUPS)
REL_BUCKETS = 32
REL_MAX_DISTANCE = 1024
D_FF = 4 * D_MODEL
A_Q = A_HEADS * HEAD_DIM
A_KV = A_KV_HEADS * HEAD_DIM
B_W = B_HEADS * HEAD_DIM
B_OUT = B_HEADS_PER_GROUP * HEAD_DIM
RMS_EPS = 1e-6
NEG_INF = -1e30
LOG2E = math.log2(math.e)
LN2 = math.log(2.0)

LANES = 128
QUAD = 4 * HEAD_DIM
PAIR = 2 * HEAD_DIM
SLAB = 3 * QUAD
Q_BLOCK = 128
VMEM_LIMIT_BYTES = 58 * 1024 * 1024

A_HEAD_ORDER = (0, 4, 1, 5, 2, 6, 3, 7)

_BF16 = jnp.bfloat16
_F32 = jnp.float32


def _rel_bucket(rel):
    half = REL_BUCKETS // 2
    max_exact = half // 2
    n = np.abs(rel)
    large = max_exact + (np.log(np.maximum(n, 1) / max_exact)
                         / np.log(REL_MAX_DISTANCE / max_exact) * (half - max_exact)).astype(np.int32)
    large = np.minimum(large, half - 1)
    return (np.where(rel > 0, half, 0) + np.where(n < max_exact, n, large)).astype(np.int32)


def _bias_blocks(table, n, dist_scale, nk):
    rel = np.arange(nk)[:, None] - n - np.arange(Q_BLOCK)[None, :]
    bucket = _rel_bucket(rel * dist_scale)
    table = table.astype(_F32) * LOG2E
    bias = jnp.zeros((table.shape[1], nk, Q_BLOCK), _F32)
    for bkt in np.unique(bucket):
        bias = jnp.where(jnp.asarray(bucket == bkt)[None], table[bkt][:, None, None], bias)
    band = np.abs(rel) <= n
    key = np.arange(nk)[:, None]
    variants = []
    for var in range(4):
        ok = band.copy()
        if var & 1:
            ok &= key >= n
        if var & 2:
            ok &= key < nk - n
        variants.append(jnp.where(ok[None], bias, NEG_INF))
    return jnp.stack(variants)


def _inproj_kernel(x_ref, g_ref, w_ref, b_ref, qa_ref, ka_ref, va_ref, zb1_ref, zb2_ref, zb3_ref,
                   gate_ref, zs_ref, *, tm):
    x = x_ref[...]
    ms = jnp.mean(x * x, axis=-1, keepdims=True)
    h = (x * lax.rsqrt(ms + RMS_EPS) * g_ref[...]).astype(_BF16)
    zg = jnp.dot(h, w_ref[:, 4 * SLAB:], preferred_element_type=_F32) + b_ref[...]
    gate_ref[...] = (0.5 * jnp.tanh(0.5 * zg) + 0.5).astype(_BF16)
    for gi in reversed(range(len(B_GROUPS))):
        out, dil = (zb1_ref, zb2_ref, zb3_ref)[gi], B_GROUPS[gi][1]
        z = jnp.dot(h, w_ref[:, (gi + 1) * SLAB:(gi + 2) * SLAB], preferred_element_type=_F32)
        if dil == 1:
            for part in range(3):
                out[0, part] = z[:, part * QUAD:(part + 1) * QUAD].astype(_BF16)
            continue
        for k in range(SLAB // LANES):
            zs_ref[k] = z[:, k * LANES:(k + 1) * LANES]
        for r in range(dil):
            for k in range(SLAB // LANES):
                part, half = divmod(k, QUAD // LANES)
                out[r, part, :, half * LANES:(half + 1) * LANES] = (
                    zs_ref[k, pl.ds(r, tm // dil, stride=dil), :].astype(_BF16))
    za = jnp.dot(h, w_ref[:, :SLAB], preferred_element_type=_F32).astype(_BF16)
    qa_ref[...] = za[:, :A_Q]
    ka_ref[...] = za[:, A_Q:A_Q + A_KV]
    va_ref[...] = za[:, A_Q + A_KV:]


def _inproj(x, g_mix, w_cat, b_gate, tm):
    b, s, _ = x.shape
    const = lambda bi, i: (0, 0)
    row = lambda bi, i: (bi, i, 0)
    res_specs = [pl.BlockSpec((None, dil, 3, tm // dil, QUAD), lambda bi, i: (bi, 0, 0, i, 0))
                 for _, dil in B_GROUPS]
    res_shapes = [jax.ShapeDtypeStruct((b, dil, 3, s // dil, QUAD), _BF16) for _, dil in B_GROUPS]
    a_widths = (A_Q, A_KV, A_KV)
    return pl.pallas_call(
        functools.partial(_inproj_kernel, tm=tm),
        grid=(b, s // tm),
        in_specs=[
            pl.BlockSpec((None, tm, D_MODEL), row),
            pl.BlockSpec((1, D_MODEL), const),
            pl.BlockSpec(w_cat.shape, const, pipeline_mode=pl.Buffered(1)),
            pl.BlockSpec((1, 2 * D_MODEL), const),
        ],
        out_specs=[pl.BlockSpec((None, tm, w), row) for w in a_widths] + res_specs
        + [pl.BlockSpec((None, tm, 2 * D_MODEL), row)],
        out_shape=[jax.ShapeDtypeStruct((b, s, w), _BF16) for w in a_widths] + res_shapes
        + [jax.ShapeDtypeStruct((b, s, 2 * D_MODEL), _BF16)],
        scratch_shapes=[pltpu.VMEM((SLAB // LANES, tm, LANES), _F32)],
        compiler_params=pltpu.CompilerParams(
            dimension_semantics=("arbitrary", "arbitrary"), vmem_limit_bytes=VMEM_LIMIT_BYTES),
        name="inproj",
    )(x, g_mix, w_cat, b_gate)


def _attn_kernel(*refs, halo, lc, nres, nquad, kv_pairs, has_sink, want_lse, chunk_axis):
    it = iter(refs)
    q_ref, kc_ref, vc_ref, kp_ref, vp_ref, kn_ref, vn_ref, bias_ref = (next(it) for _ in range(8))
    sink_ref = next(it) if has_sink else None
    o_ref = next(it)
    lse_ref = next(it) if want_lse else None
    kext_ref, vext_ref, vt_ref = next(it), next(it), next(it)

    nk = Q_BLOCK + 2 * halo
    nblk = lc // Q_BLOCK
    c = pl.program_id(chunk_axis)
    first = jnp.where(c == 0, 1, 0)
    last = jnp.where(c == pl.num_programs(chunk_axis) - 1, 2, 0)

    qgrp = lax.broadcasted_iota(jnp.int32, (Q_BLOCK, QUAD), 1) // HEAD_DIM

    def stage(r):
        pieces = ((0, halo, kp_ref, vp_ref), (halo, lc, kc_ref, vc_ref),
                  (halo + lc, halo, kn_ref, vn_ref))
        for lo, rows, k_src, v_src in pieces:
            k = k_src[r]
            kext_ref[r, lo:lo + rows, :] = k if kv_pairs == 2 else jnp.concatenate([k, k], axis=-1)
            vext_ref[r, lo:lo + rows, :] = v_src[r]
        vt_ref[r] = vext_ref[r].astype(_F32).T.astype(_BF16)

    def scores(r, i, quad):
        if i == 0 and quad == 0:
            stage(r)
        r0 = i * Q_BLOCK
        q4 = q_ref[r, r0:r0 + Q_BLOCK, quad * QUAD:(quad + 1) * QUAD]
        qs = jnp.concatenate(
            [jnp.where(qgrp == j, q4, jnp.zeros_like(q4)) for j in range(4)], axis=0)
        return lax.dot_general(kext_ref[r, r0:r0 + nk, :], qs, (((1,), (1,)), ((), ())),
                               preferred_element_type=_F32)

    def finish(r, i, quad, st):
        r0 = i * Q_BLOCK
        var = (first if i == 0 else 0) + (last if i == nblk - 1 else 0)
        pts, scales, lses = [], [], []
        for j in range(4):
            head = quad * 4 + j
            logits = st[:, j * Q_BLOCK:(j + 1) * Q_BLOCK] + bias_ref[var, head]
            m = jnp.max(logits, axis=0, keepdims=True)
            if has_sink:
                sink = sink_ref[head]
                m = jnp.maximum(m, sink)
            p = jnp.exp2(logits - m)
            den = jnp.sum(p, axis=0, keepdims=True)
            if has_sink:
                den = den + jnp.exp2(sink - m)
            pts.append(p.astype(_BF16))
            scales.append(1.0 / den)
            lses.append(m * LN2 + jnp.log(den))
        outs = []
        for pair in range(2):
            j0 = 2 * pair
            vrows = (pair % kv_pairs) * PAIR
            ot = jnp.dot(vt_ref[r, vrows:vrows + PAIR, r0:r0 + nk],
                         jnp.concatenate(pts[j0:j0 + 2], axis=1), preferred_element_type=_F32)
            outs.append(ot[:HEAD_DIM, :Q_BLOCK] * scales[j0])
            outs.append(ot[HEAD_DIM:, Q_BLOCK:] * scales[j0 + 1])
        o_ref[r, r0:r0 + Q_BLOCK, quad * QUAD:(quad + 1) * QUAD] = (
            jnp.concatenate(outs, axis=0).T.astype(_BF16))
        if want_lse:
            lse_t = jnp.concatenate(
                [jnp.broadcast_to(l, (HEAD_DIM, Q_BLOCK)) for l in lses], axis=0)
            lse_ref[r, r0:r0 + Q_BLOCK, :] = lse_t.T

    units = [(r, i, quad) for r in range(nres) for i in range(nblk) for quad in range(nquad)]
    ahead = 2 if nquad == 1 else 1
    pending = [scores(*u) for u in units[:ahead]]
    for idx, unit in enumerate(units):
        if idx + ahead < len(units):
            pending.append(scores(*units[idx + ahead]))
        finish(*unit, pending.pop(0))


UNITS_PER_STEP = 16


def _attn_call(q, k, v, bias, sink, *, halo, lc, nres, want_lse, name):
    def operand(x):
        arr, part = x if isinstance(x, tuple) else (x, None)
        width = arr.shape[-1]

        def spec(rows, row_block):
            if part is None:
                return pl.BlockSpec((None, nres, rows, width),
                                    lambda bi, r, c: (bi, r, row_block(c), 0))
            return pl.BlockSpec((None, nres, None, rows, width),
                                lambda bi, r, c: (bi, r, part, row_block(c), 0))
        return arr, spec

    (q_arr, q_spec), (k_arr, k_spec), (v_arr, v_spec) = operand(q), operand(k), operand(v)
    b, dil, length, qw, kvw = q_arr.shape[0], q_arr.shape[1], q_arr.shape[-2], q_arr.shape[-1], k_arr.shape[-1]
    per_chunk = lc // halo
    ext = lc + 2 * halo
    here = lambda c: c
    prev = lambda c: jnp.maximum(c * per_chunk - 1, 0)
    nxt = lambda c: jnp.minimum((c + 1) * per_chunk, length // halo - 1)
    in_specs = [
        q_spec(lc, here), k_spec(lc, here), v_spec(lc, here),
        k_spec(halo, prev), v_spec(halo, prev), k_spec(halo, nxt), v_spec(halo, nxt),
        pl.BlockSpec(bias.shape, lambda bi, r, c: (0, 0, 0, 0), pipeline_mode=pl.Buffered(1)),
    ]
    args = [q_arr, k_arr, v_arr, k_arr, v_arr, k_arr, v_arr, bias]
    if sink is not None:
        in_specs.append(pl.BlockSpec(memory_space=pltpu.SMEM))
        args.append(sink)
    out_spec = pl.BlockSpec((None, nres, lc, qw), lambda bi, r, c: (bi, r, c, 0))
    out_specs, out_shape = [out_spec], [jax.ShapeDtypeStruct((b, dil, length, qw), _BF16)]
    if want_lse:
        out_specs.append(out_spec)
        out_shape.append(jax.ShapeDtypeStruct((b, dil, length, qw), _F32))
    kernel = functools.partial(
        _attn_kernel, halo=halo, lc=lc, nres=nres, nquad=qw // QUAD, kv_pairs=kvw // PAIR,
        has_sink=sink is not None, want_lse=want_lse, chunk_axis=2)
    grid = (b, dil // nres, length // lc)
    return pl.pallas_call(
        kernel,
        grid=grid,
        in_specs=in_specs,
        out_specs=out_specs,
        out_shape=out_shape,
        scratch_shapes=[
            pltpu.VMEM((nres, ext, QUAD), _BF16),
            pltpu.VMEM((nres, ext, kvw), _BF16),
            pltpu.VMEM((nres, kvw, ext), _BF16),
        ],
        compiler_params=pltpu.CompilerParams(
            dimension_semantics=("arbitrary",) * len(grid), vmem_limit_bytes=VMEM_LIMIT_BYTES),
        name=name,
    )(*args)


def _attn_a(qa, ka, va, bias, sink):
    b, s, _ = qa.shape
    lc = min(s, Q_BLOCK * UNITS_PER_STEP // (A_Q // QUAD))
    seq = lambda a: a.reshape(b, 1, s, a.shape[-1])
    (o,) = _attn_call(seq(qa), seq(ka), seq(va), bias, sink, halo=A_WINDOW, lc=lc, nres=1,
                      want_lse=False, name="attn_a")
    return o.reshape(b, s, A_Q)


def _attn_b(zr, bias, window, name):
    _, dil, _, length, _ = zr.shape
    lc = min(length, 512 if dil > 1 else Q_BLOCK * UNITS_PER_STEP)
    nres = min(dil, UNITS_PER_STEP // (lc // Q_BLOCK))
    return _attn_call((zr, 0), (zr, 1), (zr, 2), bias, None, halo=window // (2 * dil), lc=lc,
                      nres=nres, want_lse=True, name=name)


def _rms(x, g):
    return x * lax.rsqrt(jnp.mean(x * x, axis=-1, keepdims=True) + RMS_EPS) * g


def _tokens(src_ref, stage_ref, tm):
    dil = src_ref.shape[0]
    if dil == 1:
        return src_ref[0].astype(_F32)
    for r in range(dil):
        for k in range(QUAD // LANES):
            stage_ref[k, pl.ds(r, tm // dil, stride=dil), :] = (
                src_ref[r, :, k * LANES:(k + 1) * LANES].astype(_F32))
    return jnp.concatenate([stage_ref[k] for k in range(QUAD // LANES)], axis=-1)


def _tail_kernel(x_ref, gate_ref, oa_ref, ob1_ref, ob2_ref, ob3_ref, l1_ref, l2_ref, l3_ref,
                 wa_ref, wb_ref, wo_ref, gm_ref, wu_ref, wd_ref, gf_ref, y_ref,
                 so2_ref, so3_ref, sl2_ref, sl3_ref, *, tm):
    outs = [_tokens(ob1_ref, None, tm), _tokens(ob2_ref, so2_ref, tm), _tokens(ob3_ref, so3_ref, tm)]
    lses = [_tokens(l1_ref, None, tm), _tokens(l2_ref, sl2_ref, tm), _tokens(l3_ref, sl3_ref, tm)]
    top = jnp.maximum(jnp.maximum(lses[0], lses[1]), lses[2])
    ws = [jnp.exp(l - top) for l in lses]
    mix = ws[0] * outs[0] + ws[1] * outs[1] + ws[2] * outs[2]
    o_b = (mix / (ws[0] + ws[1] + ws[2])).astype(_BF16)

    def mixer(rows):
        pa = jnp.dot(oa_ref[rows, :], wa_ref[...], preferred_element_type=_F32)
        pb = jnp.dot(o_b[rows], wb_ref[...], preferred_element_type=_F32)
        merged = (gate_ref[rows, :D_MODEL].astype(_F32) * pa
                  + gate_ref[rows, D_MODEL:].astype(_F32) * pb).astype(_BF16)
        x1 = x_ref[rows, :] + jnp.dot(merged, wo_ref[...], preferred_element_type=_F32)
        return x1, _rms(x1, gm_ref[...]).astype(_BF16)

    def mlp(rows, x1, h):
        acc = x1
        for c in range(D_FF // D_MODEL):
            cols = slice(c * D_MODEL, (c + 1) * D_MODEL)
            u = jnp.maximum(jnp.dot(h, wu_ref[:, cols], preferred_element_type=_F32), 0.0)
            acc = acc + jnp.dot((u * u).astype(_BF16), wd_ref[cols, :], preferred_element_type=_F32)
        y_ref[rows, :] = _rms(acc, gf_ref[...])

    halves = [slice(i * (tm // 2), (i + 1) * (tm // 2)) for i in range(2)]
    mixed = [mixer(rows) for rows in halves]
    for rows, (x1, h) in zip(halves, mixed):
        mlp(rows, x1, h)


def _tail(x, gates, o_a, o_bs, lses, wa, wb, wo, g_mlp, wu, wd, g_final, tm):
    b, s, _ = x.shape
    const = lambda bi, i: (0, 0)
    row = lambda bi, i: (bi, i, 0)
    resident = lambda a: pl.BlockSpec(a.shape, const, pipeline_mode=pl.Buffered(1))
    res_specs = [pl.BlockSpec((None, dil, tm // dil, QUAD), lambda bi, i: (bi, 0, i, 0))
                 for _, dil in B_GROUPS]
    in_specs = (
        [pl.BlockSpec((None, tm, D_MODEL), row), pl.BlockSpec((None, tm, 2 * D_MODEL), row),
         pl.BlockSpec((None, tm, A_Q), row)]
        + res_specs + res_specs
        + [resident(wa), resident(wb), resident(wo), resident(g_mlp), resident(wu), resident(wd),
           resident(g_final)]
    )
    stage = pltpu.VMEM((QUAD // LANES, tm, LANES), _F32)
    return pl.pallas_call(
        functools.partial(_tail_kernel, tm=tm),
        grid=(b, s // tm),
        in_specs=in_specs,
        out_specs=pl.BlockSpec((None, tm, D_MODEL), row),
        out_shape=jax.ShapeDtypeStruct((b, s, D_MODEL), _F32),
        scratch_shapes=[stage] * 4,
        compiler_params=pltpu.CompilerParams(
            dimension_semantics=("arbitrary", "arbitrary"), vmem_limit_bytes=VMEM_LIMIT_BYTES),
        name="tail",
    )(x, gates, o_a, *o_bs, *lses, wa, wb, wo, g_mlp, wu, wd, g_final)


def _prepare(rel_bias, g_mix, w_in, b_gate, w_branch_a, w_branch_b, w_out, attn_sink, g_mlp,
             w_up, w_down, g_final):
    scale = HEAD_DIM ** -0.5 * LOG2E
    order = np.asarray(A_HEAD_ORDER)
    o1, o2, o3 = A_Q, A_Q + A_KV, A_Q + 2 * A_KV
    o4, o5 = o3 + B_W, o3 + 2 * B_W
    o6 = o3 + 3 * B_W
    qa = w_in[:, :o1].reshape(D_MODEL, A_HEADS, HEAD_DIM)[:, order].reshape(D_MODEL, A_Q) * scale
    slabs = [qa, w_in[:, o1:o2], w_in[:, o2:o3]]
    for gi in range(len(B_GROUPS)):
        cols = slice(gi * B_OUT, (gi + 1) * B_OUT)
        slabs += [w_in[:, o3:o4][:, cols] * scale, w_in[:, o4:o5][:, cols], w_in[:, o5:o6][:, cols]]
    w_cat = jnp.concatenate(slabs + [w_in[:, o6:]], axis=1).astype(_BF16)

    wa = w_branch_a.reshape(A_HEADS, HEAD_DIM, D_MODEL)[order].reshape(A_Q, D_MODEL).astype(_BF16)
    bias_a = _bias_blocks(rel_bias[:, :A_HEADS][:, order], A_WINDOW, 1, Q_BLOCK + 2 * A_WINDOW)
    bias_b = []
    for gi, (window, dil) in enumerate(B_GROUPS):
        n = window // (2 * dil)
        table = rel_bias[:, A_HEADS + gi * B_HEADS_PER_GROUP:A_HEADS + (gi + 1) * B_HEADS_PER_GROUP]
        bias_b.append(_bias_blocks(table, n, dil, Q_BLOCK + 2 * n))
    return dict(
        w_cat=w_cat, g_mix=g_mix.reshape(1, D_MODEL), b_gate=b_gate.reshape(1, 2 * D_MODEL),
        wa=wa, wb=w_branch_b.astype(_BF16), wo=w_out.astype(_BF16),
        sink=attn_sink[order].astype(_F32) * LOG2E, g_mlp=g_mlp.reshape(1, D_MODEL),
        wu=w_up.astype(_BF16), wd=w_down.astype(_BF16), g_final=g_final.reshape(1, D_MODEL),
        bias_a=bias_a, bias_b=bias_b,
    )


def _encode(x, p):
    qa, ka, va, zb1, zb2, zb3, gates = _inproj(x, p["g_mix"], p["w_cat"], p["b_gate"], tm=512)
    o_a = _attn_a(qa, ka, va, p["bias_a"], p["sink"])
    o_bs, lses = [], []
    for gi, ((window, _), zr) in enumerate(zip(B_GROUPS, (zb1, zb2, zb3))):
        o, lse = _attn_b(zr, p["bias_b"][gi], window, f"attn_b{gi}")
        o_bs.append(o)
        lses.append(lse)
    return _tail(x, gates, o_a, o_bs, lses, p["wa"], p["wb"], p["wo"], p["g_mlp"], p["wu"], p["wd"],
                 p["g_final"], tm=512)


def kernel(x_prompt, x_sample, rel_bias, g_mix, w_in, b_gate, w_branch_a, w_branch_b, w_out,
           attn_sink, g_mlp, w_up, w_down, g_final):
    assert g_mix.shape[0] == 1, "single-layer encoder"
    p = _prepare(rel_bias, g_mix[0], w_in[0], b_gate[0], w_branch_a[0], w_branch_b[0], w_out[0],
                 attn_sink[0], g_mlp[0], w_up[0], w_down[0], g_final)
    return (_encode(x_prompt, p), _encode(x_sample, p))
```

```python
import functools
import math

import numpy as np
import jax
import jax.numpy as jnp
from jax import lax
from jax.experimental import pallas as pl
from jax.experimental.pallas import tpu as pltpu

D_MODEL = 1024
HEAD_DIM = 64
A_HEADS = 8
A_KV_HEADS = 2
A_WINDOW = 128
B_GROUPS = ((128, 1), (512, 4), (2048, 16))
B_HEADS_PER_GROUP = 4
B_HEADS = B_HEADS_PER_GROUP * len(B_GROUPS)
REL_BUCKETS = 32
REL_MAX_DISTANCE = 1024
D_FF = 4 * D_MODEL
A_Q = A_HEADS * HEAD_DIM
A_KV = A_KV_HEADS * HEAD_DIM
B_W = B_HEADS * HEAD_DIM
B_OUT = B_HEADS_PER_GROUP * HEAD_DIM
RMS_EPS = 1e-6
NEG_INF = -1e30
LOG2E = math.log2(math.e)
LN2 = math.log(2.0)

LANES = 128
QUAD = 4 * HEAD_DIM
PAIR = 2 * HEAD_DIM
SLAB = 3 * QUAD
Q_BLOCK = 128
VMEM_LIMIT_BYTES = 58 * 1024 * 1024

A_HEAD_ORDER = (0, 4, 1, 5, 2, 6, 3, 7)

_BF16 = jnp.bfloat16
_F32 = jnp.float32


def _rel_bucket(rel):
    half = REL_BUCKETS // 2
    max_exact = half // 2
    n = np.abs(rel)
    large = max_exact + (np.log(np.maximum(n, 1) / max_exact)
                         / np.log(REL_MAX_DISTANCE / max_exact) * (half - max_exact)).astype(np.int32)
    large = np.minimum(large, half - 1)
    return (np.where(rel > 0, half, 0) + np.where(n < max_exact, n, large)).astype(np.int32)


def _bias_blocks(table, n, dist_scale, nk):
    rel = np.arange(nk)[:, None] - n - np.arange(Q_BLOCK)[None, :]
    bucket = _rel_bucket(rel * dist_scale)
    table = table.astype(_F32) * LOG2E
    bias = jnp.zeros((table.shape[1], nk, Q_BLOCK), _F32)
    for bkt in np.unique(bucket):
        bias = jnp.where(jnp.asarray(bucket == bkt)[None], table[bkt][:, None, None], bias)
    band = np.abs(rel) <= n
    key = np.arange(nk)[:, None]
    variants = []
    for var in range(4):
        ok = band.copy()
        if var & 1:
            ok &= key >= n
        if var & 2:
            ok &= key < nk - n
        variants.append(jnp.where(ok[None], bias, NEG_INF))
    return jnp.stack(variants)


def _inproj_kernel(x_ref, g_ref, w_ref, b_ref, qa_ref, ka_ref, va_ref, zb1_ref, zb2_ref, zb3_ref,
                   gate_ref, zs_ref, *, tm):
    x = x_ref[...]
    ms = jnp.mean(x * x, axis=-1, keepdims=True)
    h = (x * lax.rsqrt(ms + RMS_EPS) * g_ref[...]).astype(_BF16)
    zg = jnp.dot(h, w_ref[:, 4 * SLAB:], preferred_element_type=_F32) + b_ref[...]
    gate_ref[...] = (0.5 * jnp.tanh(0.5 * zg) + 0.5).astype(_BF16)
    for gi in reversed(range(len(B_GROUPS))):
        out, dil = (zb1_ref, zb2_ref, zb3_ref)[gi], B_GROUPS[gi][1]
        z = jnp.dot(h, w_ref[:, (gi + 1) * SLAB:(gi + 2) * SLAB], preferred_element_type=_F32)
        if dil == 1:
            for part in range(3):
                out[0, part] = z[:, part * QUAD:(part + 1) * QUAD].astype(_BF16)
            continue
        for k in range(SLAB // LANES):
            zs_ref[k] = z[:, k * LANES:(k + 1) * LANES]
        for r in range(dil):
            for k in range(SLAB // LANES):
                part, half = divmod(k, QUAD // LANES)
                out[r, part, :, half * LANES:(half + 1) * LANES] = (
                    zs_ref[k, pl.ds(r, tm // dil, stride=dil), :].astype(_BF16))
    za = jnp.dot(h, w_ref[:, :SLAB], preferred_element_type=_F32).astype(_BF16)
    qa_ref[...] = za[:, :A_Q]
    ka_ref[...] = za[:, A_Q:A_Q + A_KV]
    va_ref[...] = za[:, A_Q + A_KV:]


def _inproj(x, g_mix, w_cat, b_gate, tm):
    b, s, _ = x.shape
    const = lambda bi, i: (0, 0)
    row = lambda bi, i: (bi, i, 0)
    res_specs = [pl.BlockSpec((None, dil, 3, tm // dil, QUAD), lambda bi, i: (bi, 0, 0, i, 0))
                 for _, dil in B_GROUPS]
    res_shapes = [jax.ShapeDtypeStruct((b, dil, 3, s // dil, QUAD), _BF16) for _, dil in B_GROUPS]
    a_widths = (A_Q, A_KV, A_KV)
    return pl.pallas_call(
        functools.partial(_inproj_kernel, tm=tm),
        grid=(b, s // tm),
        in_specs=[
            pl.BlockSpec((None, tm, D_MODEL), row),
            pl.BlockSpec((1, D_MODEL), const),
            pl.BlockSpec(w_cat.shape, const, pipeline_mode=pl.Buffered(1)),
            pl.BlockSpec((1, 2 * D_MODEL), const),
        ],
        out_specs=[pl.BlockSpec((None, tm, w), row) for w in a_widths] + res_specs
        + [pl.BlockSpec((None, tm, 2 * D_MODEL), row)],
        out_shape=[jax.ShapeDtypeStruct((b, s, w), _BF16) for w in a_widths] + res_shapes
        + [jax.ShapeDtypeStruct((b, s, 2 * D_MODEL), _BF16)],
        scratch_shapes=[pltpu.VMEM((SLAB // LANES, tm, LANES), _F32)],
        compiler_params=pltpu.CompilerParams(
            dimension_semantics=("arbitrary", "arbitrary"), vmem_limit_bytes=VMEM_LIMIT_BYTES),
        name="inproj",
    )(x, g_mix, w_cat, b_gate)


def _attn_ops(q_ref, kv_refs, bias_ref, sink_ref, stage_refs, write, *, halo, lc, kv_pairs,
              first, last, want_lse):
    kc_ref, vc_ref, kp_ref, vp_ref, kn_ref, vn_ref = kv_refs
    kext_ref, vext_ref, vt_ref = stage_refs
    nk = Q_BLOCK + 2 * halo
    nblk = lc // Q_BLOCK
    qgrp = lax.broadcasted_iota(jnp.int32, (Q_BLOCK, QUAD), 1) // HEAD_DIM

    def stage(r):
        pieces = ((0, halo, kp_ref, vp_ref), (halo, lc, kc_ref, vc_ref),
                  (halo + lc, halo, kn_ref, vn_ref))
        for lo, rows, k_src, v_src in pieces:
            k = k_src[r]
            kext_ref[r, lo:lo + rows, :] = k if kv_pairs == 2 else jnp.concatenate([k, k], axis=-1)
            vext_ref[r, lo:lo + rows, :] = v_src[r]
        vt_ref[r] = vext_ref[r].astype(_F32).T.astype(_BF16)

    def scores(r, i, quad):
        if i == 0 and quad == 0:
            stage(r)
        r0 = i * Q_BLOCK
        q4 = q_ref[r, r0:r0 + Q_BLOCK, quad * QUAD:(quad + 1) * QUAD]
        qs = jnp.concatenate(
            [jnp.where(qgrp == j, q4, jnp.zeros_like(q4)) for j in range(4)], axis=0)
        return lax.dot_general(kext_ref[r, r0:r0 + nk, :], qs, (((1,), (1,)), ((), ())),
                               preferred_element_type=_F32)

    def finish(r, i, quad, st):
        r0 = i * Q_BLOCK
        var = (first if i == 0 else 0) + (last if i == nblk - 1 else 0)
        pts, scales, lses = [], [], []
        for j in range(4):
            head = quad * 4 + j
            logits = st[:, j * Q_BLOCK:(j + 1) * Q_BLOCK] + bias_ref[var, head]
            m = jnp.max(logits, axis=0, keepdims=True)
            if sink_ref is not None:
                sink = sink_ref[head]
                m = jnp.maximum(m, sink)
            p = jnp.exp2(logits - m)
            den = jnp.sum(p, axis=0, keepdims=True)
            if sink_ref is not None:
                den = den + jnp.exp2(sink - m)
            pts.append(p.astype(_BF16))
            scales.append(1.0 / den)
            lses.append(m * LN2 + jnp.log(den))
        outs = []
        for pair in range(2):
            j0 = 2 * pair
            vrows = (pair % kv_pairs) * PAIR
            ot = jnp.dot(vt_ref[r, vrows:vrows + PAIR, r0:r0 + nk],
                         jnp.concatenate(pts[j0:j0 + 2], axis=1), preferred_element_type=_F32)
            outs.append(ot[:HEAD_DIM, :Q_BLOCK] * scales[j0])
            outs.append(ot[HEAD_DIM:, Q_BLOCK:] * scales[j0 + 1])
        lse = None
        if want_lse:
            lse = jnp.concatenate(
                [jnp.broadcast_to(l, (HEAD_DIM, Q_BLOCK)) for l in lses], axis=0).T
        write(r, r0, quad, jnp.concatenate(outs, axis=0).T.astype(_BF16), lse)

    return scores, finish


def _attn_kernel(*refs, halo, lc, nres, nquad, kv_pairs, has_sink, want_lse, chunk_axis):
    it = iter(refs)
    q_ref = next(it)
    kv_refs = [next(it) for _ in range(6)]
    bias_ref = next(it)
    sink_ref = next(it) if has_sink else None
    o_ref = next(it)
    lse_ref = next(it) if want_lse else None
    stage_refs = [next(it) for _ in range(3)]

    c = pl.program_id(chunk_axis)
    first = jnp.where(c == 0, 1, 0)
    last = jnp.where(c == pl.num_programs(chunk_axis) - 1, 2, 0)

    def write(r, r0, quad, o, lse):
        o_ref[r, r0:r0 + Q_BLOCK, quad * QUAD:(quad + 1) * QUAD] = o
        if want_lse:
            lse_ref[r, r0:r0 + Q_BLOCK, :] = lse

    scores, finish = _attn_ops(q_ref, kv_refs, bias_ref, sink_ref, stage_refs, write, halo=halo,
                               lc=lc, kv_pairs=kv_pairs, first=first, last=last, want_lse=want_lse)

    units = [(r, i, quad) for r in range(nres) for i in range(lc // Q_BLOCK) for quad in range(nquad)]
    ahead = 2 if nquad == 1 else 1
    pending = [scores(*u) for u in units[:ahead]]
    for idx, unit in enumerate(units):
        if idx + ahead < len(units):
            pending.append(scores(*units[idx + ahead]))
        finish(*unit, pending.pop(0))


UNITS_PER_STEP = 16


def _attn_call(q, k, v, bias, sink, *, halo, lc, nres, want_lse, name):
    def operand(x):
        arr, part = x if isinstance(x, tuple) else (x, None)
        width = arr.shape[-1]

        def spec(rows, row_block):
            if part is None:
                return pl.BlockSpec((None, nres, rows, width),
                                    lambda bi, r, c: (bi, r, row_block(c), 0))
            return pl.BlockSpec((None, nres, None, rows, width),
                                lambda bi, r, c: (bi, r, part, row_block(c), 0))
        return arr, spec

    (q_arr, q_spec), (k_arr, k_spec), (v_arr, v_spec) = operand(q), operand(k), operand(v)
    b, dil, length, qw, kvw = q_arr.shape[0], q_arr.shape[1], q_arr.shape[-2], q_arr.shape[-1], k_arr.shape[-1]
    per_chunk = lc // halo
    ext = lc + 2 * halo
    here = lambda c: c
    prev = lambda c: jnp.maximum(c * per_chunk - 1, 0)
    nxt = lambda c: jnp.minimum((c + 1) * per_chunk, length // halo - 1)
    in_specs = [
        q_spec(lc, here), k_spec(lc, here), v_spec(lc, here),
        k_spec(halo, prev), v_spec(halo, prev), k_spec(halo, nxt), v_spec(halo, nxt),
        pl.BlockSpec(bias.shape, lambda bi, r, c: (0, 0, 0, 0), pipeline_mode=pl.Buffered(1)),
    ]
    args = [q_arr, k_arr, v_arr, k_arr, v_arr, k_arr, v_arr, bias]
    if sink is not None:
        in_specs.append(pl.BlockSpec(memory_space=pltpu.SMEM))
        args.append(sink)
    out_spec = pl.BlockSpec((None, nres, lc, qw), lambda bi, r, c: (bi, r, c, 0))
    out_specs, out_shape = [out_spec], [jax.ShapeDtypeStruct((b, dil, length, qw), _BF16)]
    if want_lse:
        out_specs.append(out_spec)
        out_shape.append(jax.ShapeDtypeStruct((b, dil, length, qw), _F32))
    kernel = functools.partial(
        _attn_kernel, halo=halo, lc=lc, nres=nres, nquad=qw // QUAD, kv_pairs=kvw // PAIR,
        has_sink=sink is not None, want_lse=want_lse, chunk_axis=2)
    grid = (b, dil // nres, length // lc)
    return pl.pallas_call(
        kernel,
        grid=grid,
        in_specs=in_specs,
        out_specs=out_specs,
        out_shape=out_shape,
        scratch_shapes=[
            pltpu.VMEM((nres, ext, QUAD), _BF16),
            pltpu.VMEM((nres, ext, kvw), _BF16),
            pltpu.VMEM((nres, kvw, ext), _BF16),
        ],
        compiler_params=pltpu.CompilerParams(
            dimension_semantics=("arbitrary",) * len(grid), vmem_limit_bytes=VMEM_LIMIT_BYTES),
        name=name,
    )(*args)


def _attn_b(zr, bias, window, name):
    _, dil, _, length, _ = zr.shape
    lc = min(length, 512 if dil > 1 else Q_BLOCK * UNITS_PER_STEP)
    nres = min(dil, UNITS_PER_STEP // (lc // Q_BLOCK))
    return _attn_call((zr, 0), (zr, 1), (zr, 2), bias, None, halo=window // (2 * dil), lc=lc,
                      nres=nres, want_lse=True, name=name)


def _rms(x, g):
    return x * lax.rsqrt(jnp.mean(x * x, axis=-1, keepdims=True) + RMS_EPS) * g


def _tokens(src_ref, stage_ref, tm):
    dil = src_ref.shape[0]
    if dil == 1:
        return src_ref[0].astype(_F32)
    for r in range(dil):
        for k in range(QUAD // LANES):
            stage_ref[k, pl.ds(r, tm // dil, stride=dil), :] = (
                src_ref[r, :, k * LANES:(k + 1) * LANES].astype(_F32))
    return jnp.concatenate([stage_ref[k] for k in range(QUAD // LANES)], axis=-1)


def _tail_kernel(*refs, tm, tiles_per_seq, n_tiles):
    it = iter(refs)
    qa_ref = next(it)
    kv_refs = [next(it) for _ in range(6)]
    bias_ref, sink_ref = next(it), next(it)
    x_ref, gate_ref = next(it), next(it)
    ob_refs = [next(it) for _ in range(3)]
    l_refs = [next(it) for _ in range(3)]
    wa_ref, wb_ref, wo_ref, gm_ref, wu_ref, wd_ref, gf_ref = (next(it) for _ in range(7))
    y_ref = next(it)
    so2_ref, so3_ref, sl2_ref, sl3_ref = (next(it) for _ in range(4))
    stage_refs = [next(it) for _ in range(3)]
    oa_ref = next(it)

    k = pl.program_id(0)
    seq_tile = jnp.minimum(k, n_tiles - 1) % tiles_per_seq
    first = jnp.where(seq_tile == 0, 1, 0)
    last = jnp.where(seq_tile == tiles_per_seq - 1, 2, 0)
    slot_w = k % 2
    slot_r = 1 - slot_w

    @pl.when(k == 0)
    def _():
        oa_ref[1] = jnp.zeros(oa_ref.shape[1:], _BF16)

    def write(r, r0, quad, o, lse):
        oa_ref[slot_w, r0:r0 + Q_BLOCK, quad * QUAD:(quad + 1) * QUAD] = o

    scores, finish = _attn_ops(qa_ref, kv_refs, bias_ref, sink_ref, stage_refs, write,
                               halo=A_WINDOW, lc=tm, kv_pairs=1, first=first, last=last,
                               want_lse=False)
    units = [(0, i, quad) for i in range(tm // Q_BLOCK) for quad in range(A_Q // QUAD)]

    outs = [_tokens(ob_refs[0], None, tm), _tokens(ob_refs[1], so2_ref, tm),
            _tokens(ob_refs[2], so3_ref, tm)]
    lses = [_tokens(l_refs[0], None, tm), _tokens(l_refs[1], sl2_ref, tm),
            _tokens(l_refs[2], sl3_ref, tm)]
    top = jnp.maximum(jnp.maximum(lses[0], lses[1]), lses[2])
    ws = [jnp.exp(l - top) for l in lses]
    mix = ws[0] * outs[0] + ws[1] * outs[1] + ws[2] * outs[2]
    o_b = (mix / (ws[0] + ws[1] + ws[2])).astype(_BF16)

    def mixer(rows):
        pa = jnp.dot(oa_ref[slot_r, rows, :], wa_ref[...], preferred_element_type=_F32)
        pb = jnp.dot(o_b[rows], wb_ref[...], preferred_element_type=_F32)
        merged = (gate_ref[rows, :D_MODEL].astype(_F32) * pa
                  + gate_ref[rows, D_MODEL:].astype(_F32) * pb).astype(_BF16)
        x1 = x_ref[rows, :] + jnp.dot(merged, wo_ref[...], preferred_element_type=_F32)
        return x1, _rms(x1, gm_ref[...]).astype(_BF16)

    halves = [slice(i * (tm // 2), (i + 1) * (tm // 2)) for i in range(2)]
    st = scores(*units[0])
    mixed = [mixer(rows) for rows in halves]
    done = 0
    for rows, (x1, h) in zip(halves, mixed):
        acc = x1
        for c in range(D_FF // D_MODEL):
            cols = slice(c * D_MODEL, (c + 1) * D_MODEL)
            u = jnp.maximum(jnp.dot(h, wu_ref[:, cols], preferred_element_type=_F32), 0.0)
            acc = acc + jnp.dot((u * u).astype(_BF16), wd_ref[cols, :], preferred_element_type=_F32)
            if done < len(units):
                st_next = scores(*units[done + 1]) if done + 1 < len(units) else None
                finish(*units[done], st)
                st, done = st_next, done + 1
        y_ref[rows, :] = _rms(acc, gf_ref[...])
    assert done == len(units)


def _tail(x, gates, qa, ka, va, bias_a, sink, o_bs, lses, wa, wb, wo, g_mlp, wu, wd, g_final, tm):
    b, s, _ = x.shape
    tiles_per_seq = s // tm
    n_tiles = b * tiles_per_seq
    per_tile = tm // A_WINDOW

    def attn_tile(k):
        t = jnp.minimum(k, n_tiles - 1)
        return t // tiles_per_seq, t % tiles_per_seq

    def tail_tile(k):
        t = jnp.maximum(k - 1, 0)
        return t // tiles_per_seq, t % tiles_per_seq

    def attn_spec(width, rows, row_block):
        def index(k):
            bi, i = attn_tile(k)
            return bi, 0, row_block(i), 0
        return pl.BlockSpec((None, 1, rows, width), index)

    here = lambda i: i
    prev = lambda i: jnp.maximum(i * per_tile - 1, 0)
    nxt = lambda i: jnp.minimum((i + 1) * per_tile, s // A_WINDOW - 1)
    resident = lambda a: pl.BlockSpec(a.shape, lambda k: (0,) * a.ndim, pipeline_mode=pl.Buffered(1))
    row = lambda k: tail_tile(k) + (0,)
    res_specs = [pl.BlockSpec((None, dil, tm // dil, QUAD),
                              lambda k: (tail_tile(k)[0], 0, tail_tile(k)[1], 0))
                 for _, dil in B_GROUPS]
    in_specs = (
        [attn_spec(A_Q, tm, here), attn_spec(A_KV, tm, here), attn_spec(A_KV, tm, here),
         attn_spec(A_KV, A_WINDOW, prev), attn_spec(A_KV, A_WINDOW, prev),
         attn_spec(A_KV, A_WINDOW, nxt), attn_spec(A_KV, A_WINDOW, nxt),
         resident(bias_a), pl.BlockSpec(memory_space=pltpu.SMEM)]
        + [pl.BlockSpec((None, tm, D_MODEL), row), pl.BlockSpec((None, tm, 2 * D_MODEL), row)]
        + res_specs + res_specs
        + [resident(wa), resident(wb), resident(wo), resident(g_mlp), resident(wu), resident(wd),
           resident(g_final)]
    )
    seq = lambda a: a.reshape(b, 1, s, a.shape[-1])
    stage = pltpu.VMEM((QUAD // LANES, tm, LANES), _F32)
    ext = tm + 2 * A_WINDOW
    return pl.pallas_call(
        functools.partial(_tail_kernel, tm=tm, tiles_per_seq=tiles_per_seq, n_tiles=n_tiles),
        grid=(n_tiles + 1,),
        in_specs=in_specs,
        out_specs=pl.BlockSpec((None, tm, D_MODEL), row),
        out_shape=jax.ShapeDtypeStruct((b, s, D_MODEL), _F32),
        scratch_shapes=[stage] * 4 + [
            pltpu.VMEM((1, ext, QUAD), _BF16),
            pltpu.VMEM((1, ext, A_KV), _BF16),
            pltpu.VMEM((1, A_KV, ext), _BF16),
            pltpu.VMEM((2, tm, A_Q), _BF16),
        ],
        compiler_params=pltpu.CompilerParams(
            dimension_semantics=("arbitrary",), vmem_limit_bytes=VMEM_LIMIT_BYTES),
        name="tail",
    )(seq(qa), seq(ka), seq(va), seq(ka), seq(va), seq(ka), seq(va), bias_a, sink,
      x, gates, *o_bs, *lses, wa, wb, wo, g_mlp, wu, wd, g_final)


def _prepare(rel_bias, g_mix, w_in, b_gate, w_branch_a, w_branch_b, w_out, attn_sink, g_mlp,
             w_up, w_down, g_final):
    scale = HEAD_DIM ** -0.5 * LOG2E
    order = np.asarray(A_HEAD_ORDER)
    o1, o2, o3 = A_Q, A_Q + A_KV, A_Q + 2 * A_KV
    o4, o5 = o3 + B_W, o3 + 2 * B_W
    o6 = o3 + 3 * B_W
    qa = w_in[:, :o1].reshape(D_MODEL, A_HEADS, HEAD_DIM)[:, order].reshape(D_MODEL, A_Q) * scale
    slabs = [qa, w_in[:, o1:o2], w_in[:, o2:o3]]
    for gi in range(len(B_GROUPS)):
        cols = slice(gi * B_OUT, (gi + 1) * B_OUT)
        slabs += [w_in[:, o3:o4][:, cols] * scale, w_in[:, o4:o5][:, cols], w_in[:, o5:o6][:, cols]]
    w_cat = jnp.concatenate(slabs + [w_in[:, o6:]], axis=1).astype(_BF16)

    wa = w_branch_a.reshape(A_HEADS, HEAD_DIM, D_MODEL)[order].reshape(A_Q, D_MODEL).astype(_BF16)
    bias_a = _bias_blocks(rel_bias[:, :A_HEADS][:, order], A_WINDOW, 1, Q_BLOCK + 2 * A_WINDOW)
    bias_b = []
    for gi, (window, dil) in enumerate(B_GROUPS):
        n = window // (2 * dil)
        table = rel_bias[:, A_HEADS + gi * B_HEADS_PER_GROUP:A_HEADS + (gi + 1) * B_HEADS_PER_GROUP]
        bias_b.append(_bias_blocks(table, n, dil, Q_BLOCK + 2 * n))
    return dict(
        w_cat=w_cat, g_mix=g_mix.reshape(1, D_MODEL), b_gate=b_gate.reshape(1, 2 * D_MODEL),
        wa=wa, wb=w_branch_b.astype(_BF16), wo=w_out.astype(_BF16),
        sink=attn_sink[order].astype(_F32) * LOG2E, g_mlp=g_mlp.reshape(1, D_MODEL),
        wu=w_up.astype(_BF16), wd=w_down.astype(_BF16), g_final=g_final.reshape(1, D_MODEL),
        bias_a=bias_a, bias_b=bias_b,
    )


def _encode(x, p):
    qa, ka, va, zb1, zb2, zb3, gates = _inproj(x, p["g_mix"], p["w_cat"], p["b_gate"], tm=512)
    o_bs, lses = [], []
    for gi, ((window, _), zr) in enumerate(zip(B_GROUPS, (zb1, zb2, zb3))):
        o, lse = _attn_b(zr, p["bias_b"][gi], window, f"attn_b{gi}")
        o_bs.append(o)
        lses.append(lse)
    return _tail(x, gates, qa, ka, va, p["bias_a"], p["sink"], o_bs, lses, p["wa"], p["wb"], p["wo"],
                 p["g_mlp"], p["wu"], p["wd"], p["g_final"], tm=512)


def kernel(x_prompt, x_sample, rel_bias, g_mix, w_in, b_gate, w_branch_a, w_branch_b, w_out,
           attn_sink, g_mlp, w_up, w_down, g_final):
    assert g_mix.shape[0] == 1, "single-layer encoder"
    p = _prepare(rel_bias, g_mix[0], w_in[0], b_gate[0], w_branch_a[0], w_branch_b[0], w_out[0],
                 attn_sink[0], g_mlp[0], w_up[0], w_down[0], g_final)
    return (_encode(x_prompt, p), _encode(x_sample, p))
```

```python
import functools
import math

import numpy as np
import jax
import jax.numpy as jnp
from jax import lax
from jax.experimental import pallas as pl
from jax.experimental.pallas import tpu as pltpu

D_MODEL = 1024
HEAD_DIM = 64
A_HEADS = 8
A_KV_HEADS = 2
A_WINDOW = 128
B_GROUPS = ((128, 1), (512, 4), (2048, 16))
B_HEADS_PER_GROUP = 4
B_HEADS = B_HEADS_PER_GROUP * len(B_GROUPS)
REL_BUCKETS = 32
REL_MAX_DISTANCE = 1024
D_FF = 4 * D_MODEL
A_Q = A_HEADS * HEAD_DIM
A_KV = A_KV_HEADS * HEAD_DIM
B_W = B_HEADS * HEAD_DIM
B_OUT = B_HEADS_PER_GROUP * HEAD_DIM
RMS_EPS = 1e-6
NEG_INF = -1e30
LOG2E = math.log2(math.e)
LN2 = math.log(2.0)

LANES = 128
QUAD = 4 * HEAD_DIM
PAIR = 2 * HEAD_DIM
SLAB = 3 * QUAD
Q_BLOCK = 128
SUB_ROWS = 256
VMEM_LIMIT_BYTES = 58 * 1024 * 1024

A_HEAD_ORDER = (0, 4, 1, 5, 2, 6, 3, 7)

_BF16 = jnp.bfloat16
_F32 = jnp.float32


def _rel_bucket(rel):
    half = REL_BUCKETS // 2
    max_exact = half // 2
    n = np.abs(rel)
    large = max_exact + (np.log(np.maximum(n, 1) / max_exact)
                         / np.log(REL_MAX_DISTANCE / max_exact) * (half - max_exact)).astype(np.int32)
    large = np.minimum(large, half - 1)
    return (np.where(rel > 0, half, 0) + np.where(n < max_exact, n, large)).astype(np.int32)


def _bias_blocks(table, n, dist_scale, nk):
    rel = np.arange(nk)[:, None] - n - np.arange(Q_BLOCK)[None, :]
    bucket = _rel_bucket(rel * dist_scale)
    table = table.astype(_F32) * LOG2E
    bias = jnp.zeros((table.shape[1], nk, Q_BLOCK), _F32)
    for bkt in np.unique(bucket):
        bias = jnp.where(jnp.asarray(bucket == bkt)[None], table[bkt][:, None, None], bias)
    band = np.abs(rel) <= n
    key = np.arange(nk)[:, None]
    variants = []
    for var in range(4):
        ok = band.copy()
        if var & 1:
            ok &= key >= n
        if var & 2:
            ok &= key < nk - n
        variants.append(jnp.where(ok[None], bias, NEG_INF))
    return jnp.stack(variants)


def _inproj_kernel(x_ref, g_ref, w_ref, b_ref, qa_ref, ka_ref, va_ref, zb1_ref, zb2_ref, zb3_ref,
                   gate_ref, zs_ref, *, tm):
    x = x_ref[...]
    ms = jnp.mean(x * x, axis=-1, keepdims=True)
    h = (x * lax.rsqrt(ms + RMS_EPS) * g_ref[...]).astype(_BF16)
    for sub in range(tm // SUB_ROWS):
        rows = slice(sub * SUB_ROWS, (sub + 1) * SUB_ROWS)
        hs = h[rows]
        zg = jnp.dot(hs, w_ref[:, 4 * SLAB:], preferred_element_type=_F32) + b_ref[...]
        gate_ref[rows, :] = (0.5 * jnp.tanh(0.5 * zg) + 0.5).astype(_BF16)
        for gi in reversed(range(len(B_GROUPS))):
            out, dil = (zb1_ref, zb2_ref, zb3_ref)[gi], B_GROUPS[gi][1]
            z = jnp.dot(hs, w_ref[:, (gi + 1) * SLAB:(gi + 2) * SLAB], preferred_element_type=_F32)
            if dil == 1:
                for part in range(3):
                    out[0, part, rows, :] = z[:, part * QUAD:(part + 1) * QUAD].astype(_BF16)
                continue
            for k in range(SLAB // LANES):
                zs_ref[k, rows, :] = z[:, k * LANES:(k + 1) * LANES]
            per = SUB_ROWS // dil
            for r in range(dil):
                for k in range(SLAB // LANES):
                    part, half = divmod(k, QUAD // LANES)
                    out[r, part, sub * per:(sub + 1) * per, half * LANES:(half + 1) * LANES] = (
                        zs_ref[k, pl.ds(sub * SUB_ROWS + r, per, stride=dil), :].astype(_BF16))
        za = jnp.dot(hs, w_ref[:, :SLAB], preferred_element_type=_F32).astype(_BF16)
        qa_ref[rows, :] = za[:, :A_Q]
        ka_ref[rows, :] = za[:, A_Q:A_Q + A_KV]
        va_ref[rows, :] = za[:, A_Q + A_KV:]


def _inproj(x, g_mix, w_cat, b_gate, tm):
    b, s, _ = x.shape
    const = lambda bi, i: (0, 0)
    row = lambda bi, i: (bi, i, 0)
    res_specs = [pl.BlockSpec((None, dil, 3, tm // dil, QUAD), lambda bi, i: (bi, 0, 0, i, 0))
                 for _, dil in B_GROUPS]
    res_shapes = [jax.ShapeDtypeStruct((b, dil, 3, s // dil, QUAD), _BF16) for _, dil in B_GROUPS]
    a_widths = (A_Q, A_KV, A_KV)
    return pl.pallas_call(
        functools.partial(_inproj_kernel, tm=tm),
        grid=(b, s // tm),
        in_specs=[
            pl.BlockSpec((None, tm, D_MODEL), row),
            pl.BlockSpec((1, D_MODEL), const),
            pl.BlockSpec(w_cat.shape, const, pipeline_mode=pl.Buffered(1)),
            pl.BlockSpec((1, 2 * D_MODEL), const),
        ],
        out_specs=[pl.BlockSpec((None, tm, w), row) for w in a_widths] + res_specs
        + [pl.BlockSpec((None, tm, 2 * D_MODEL), row)],
        out_shape=[jax.ShapeDtypeStruct((b, s, w), _BF16) for w in a_widths] + res_shapes
        + [jax.ShapeDtypeStruct((b, s, 2 * D_MODEL), _BF16)],
        scratch_shapes=[pltpu.VMEM((SLAB // LANES, tm, LANES), _F32)],
        compiler_params=pltpu.CompilerParams(
            dimension_semantics=("arbitrary", "arbitrary"), vmem_limit_bytes=VMEM_LIMIT_BYTES),
        name="inproj",
    )(x, g_mix, w_cat, b_gate)


def _attn_ops(q_ref, kv_refs, bias_ref, sink_ref, stage_refs, write, *, halo, lc, kv_pairs,
              first, last, want_lse):
    kc_ref, vc_ref, kp_ref, vp_ref, kn_ref, vn_ref = kv_refs
    kext_ref, vext_ref, vt_ref = stage_refs
    nk = Q_BLOCK + 2 * halo
    nblk = lc // Q_BLOCK
    qgrp = lax.broadcasted_iota(jnp.int32, (Q_BLOCK, QUAD), 1) // HEAD_DIM

    def stage(r):
        pieces = ((0, halo, kp_ref, vp_ref), (halo, lc, kc_ref, vc_ref),
                  (halo + lc, halo, kn_ref, vn_ref))
        for lo, rows, k_src, v_src in pieces:
            k = k_src[r]
            kext_ref[r, lo:lo + rows, :] = k if kv_pairs == 2 else jnp.concatenate([k, k], axis=-1)
            vext_ref[r, lo:lo + rows, :] = v_src[r]
        vt_ref[r] = vext_ref[r].astype(_F32).T.astype(_BF16)

    def scores(r, i, quad):
        if i == 0 and quad == 0:
            stage(r)
        r0 = i * Q_BLOCK
        q4 = q_ref[r, r0:r0 + Q_BLOCK, quad * QUAD:(quad + 1) * QUAD]
        qs = jnp.concatenate(
            [jnp.where(qgrp == j, q4, jnp.zeros_like(q4)) for j in range(4)], axis=0)
        return lax.dot_general(kext_ref[r, r0:r0 + nk, :], qs, (((1,), (1,)), ((), ())),
                               preferred_element_type=_F32)

    def finish(r, i, quad, st):
        r0 = i * Q_BLOCK
        var = (first if i == 0 else 0) + (last if i == nblk - 1 else 0)
        pts, scales, lses = [], [], []
        for j in range(4):
            head = quad * 4 + j
            logits = st[:, j * Q_BLOCK:(j + 1) * Q_BLOCK] + bias_ref[var, head]
            m = jnp.max(logits, axis=0, keepdims=True)
            if sink_ref is not None:
                sink = sink_ref[head]
                m = jnp.maximum(m, sink)
            p = jnp.exp2(logits - m)
            den = jnp.sum(p, axis=0, keepdims=True)
            if sink_ref is not None:
                den = den + jnp.exp2(sink - m)
            pts.append(p.astype(_BF16))
            scales.append(1.0 / den)
            lses.append(m * LN2 + jnp.log(den))
        outs = []
        for pair in range(2):
            j0 = 2 * pair
            vrows = (pair % kv_pairs) * PAIR
            ot = jnp.dot(vt_ref[r, vrows:vrows + PAIR, r0:r0 + nk],
                         jnp.concatenate(pts[j0:j0 + 2], axis=1), preferred_element_type=_F32)
            outs.append(ot[:HEAD_DIM, :Q_BLOCK] * scales[j0])
            outs.append(ot[HEAD_DIM:, Q_BLOCK:] * scales[j0 + 1])
        lse = None
        if want_lse:
            lse = jnp.concatenate(
                [jnp.broadcast_to(l, (HEAD_DIM, Q_BLOCK)) for l in lses], axis=0).T
        write(r, r0, quad, jnp.concatenate(outs, axis=0).T.astype(_BF16), lse)

    return scores, finish


def _attn_kernel(*refs, halo, lc, nres, nquad, kv_pairs, has_sink, want_lse, chunk_axis):
    it = iter(refs)
    q_ref = next(it)
    kv_refs = [next(it) for _ in range(6)]
    bias_ref = next(it)
    sink_ref = next(it) if has_sink else None
    o_ref = next(it)
    lse_ref = next(it) if want_lse else None
    stage_refs = [next(it) for _ in range(3)]

    c = pl.program_id(chunk_axis)
    first = jnp.where(c == 0, 1, 0)
    last = jnp.where(c == pl.num_programs(chunk_axis) - 1, 2, 0)

    def write(r, r0, quad, o, lse):
        o_ref[r, r0:r0 + Q_BLOCK, quad * QUAD:(quad + 1) * QUAD] = o
        if want_lse:
            lse_ref[r, r0:r0 + Q_BLOCK, :] = lse

    scores, finish = _attn_ops(q_ref, kv_refs, bias_ref, sink_ref, stage_refs, write, halo=halo,
                               lc=lc, kv_pairs=kv_pairs, first=first, last=last, want_lse=want_lse)

    units = [(r, i, quad) for r in range(nres) for i in range(lc // Q_BLOCK) for quad in range(nquad)]
    ahead = 2 if nquad == 1 else 1
    pending = [scores(*u) for u in units[:ahead]]
    for idx, unit in enumerate(units):
        if idx + ahead < len(units):
            pending.append(scores(*units[idx + ahead]))
        finish(*unit, pending.pop(0))


UNITS_PER_STEP = 32


def _attn_call(q, k, v, bias, sink, *, halo, lc, nres, want_lse, name):
    def operand(x):
        arr, part = x if isinstance(x, tuple) else (x, None)
        width = arr.shape[-1]

        def spec(rows, row_block):
            if part is None:
                return pl.BlockSpec((None, nres, rows, width),
                                    lambda bi, r, c: (bi, r, row_block(c), 0))
            return pl.BlockSpec((None, nres, None, rows, width),
                                lambda bi, r, c: (bi, r, part, row_block(c), 0))
        return arr, spec

    (q_arr, q_spec), (k_arr, k_spec), (v_arr, v_spec) = operand(q), operand(k), operand(v)
    b, dil, length, qw, kvw = q_arr.shape[0], q_arr.shape[1], q_arr.shape[-2], q_arr.shape[-1], k_arr.shape[-1]
    per_chunk = lc // halo
    ext = lc + 2 * halo
    here = lambda c: c
    prev = lambda c: jnp.maximum(c * per_chunk - 1, 0)
    nxt = lambda c: jnp.minimum((c + 1) * per_chunk, length // halo - 1)
    in_specs = [
        q_spec(lc, here), k_spec(lc, here), v_spec(lc, here),
        k_spec(halo, prev), v_spec(halo, prev), k_spec(halo, nxt), v_spec(halo, nxt),
        pl.BlockSpec(bias.shape, lambda bi, r, c: (0, 0, 0, 0), pipeline_mode=pl.Buffered(1)),
    ]
    args = [q_arr, k_arr, v_arr, k_arr, v_arr, k_arr, v_arr, bias]
    if sink is not None:
        in_specs.append(pl.BlockSpec(memory_space=pltpu.SMEM))
        args.append(sink)
    out_spec = pl.BlockSpec((None, nres, lc, qw), lambda bi, r, c: (bi, r, c, 0))
    out_specs, out_shape = [out_spec], [jax.ShapeDtypeStruct((b, dil, length, qw), _BF16)]
    if want_lse:
        out_specs.append(out_spec)
        out_shape.append(jax.ShapeDtypeStruct((b, dil, length, qw), _F32))
    kernel = functools.partial(
        _attn_kernel, halo=halo, lc=lc, nres=nres, nquad=qw // QUAD, kv_pairs=kvw // PAIR,
        has_sink=sink is not None, want_lse=want_lse, chunk_axis=2)
    grid = (b, dil // nres, length // lc)
    return pl.pallas_call(
        kernel,
        grid=grid,
        in_specs=in_specs,
        out_specs=out_specs,
        out_shape=out_shape,
        scratch_shapes=[
            pltpu.VMEM((nres, ext, QUAD), _BF16),
            pltpu.VMEM((nres, ext, kvw), _BF16),
            pltpu.VMEM((nres, kvw, ext), _BF16),
        ],
        compiler_params=pltpu.CompilerParams(
            dimension_semantics=("arbitrary",) * len(grid), vmem_limit_bytes=VMEM_LIMIT_BYTES),
        name=name,
    )(*args)


def _attn_a(qa, ka, va, bias, sink):
    b, s, _ = qa.shape
    lc = min(s, Q_BLOCK * UNITS_PER_STEP // (A_Q // QUAD))
    seq = lambda a: a.reshape(b, 1, s, a.shape[-1])
    (o,) = _attn_call(seq(qa), seq(ka), seq(va), bias, sink, halo=A_WINDOW, lc=lc, nres=1,
                      want_lse=False, name="attn_a")
    return o.reshape(b, s, A_Q)


def _attn_b(zr, bias, window, name):
    _, dil, _, length, _ = zr.shape
    lc = min(length, 1024 if dil > 1 else Q_BLOCK * UNITS_PER_STEP)
    nres = min(dil, UNITS_PER_STEP // (lc // Q_BLOCK))
    return _attn_call((zr, 0), (zr, 1), (zr, 2), bias, None, halo=window // (2 * dil), lc=lc,
                      nres=nres, want_lse=True, name=name)


def _rms(x, g):
    return x * lax.rsqrt(jnp.mean(x * x, axis=-1, keepdims=True) + RMS_EPS) * g


def _tokens(src_ref, stage_ref, tm):
    dil = src_ref.shape[0]
    if dil == 1:
        return src_ref[0].astype(_F32)
    for r in range(dil):
        for k in range(QUAD // LANES):
            stage_ref[k, pl.ds(r, tm // dil, stride=dil), :] = (
                src_ref[r, :, k * LANES:(k + 1) * LANES].astype(_F32))
    return jnp.concatenate([stage_ref[k] for k in range(QUAD // LANES)], axis=-1)


def _tail_kernel(x_ref, gate_ref, oa_ref, ob1_ref, ob2_ref, ob3_ref, l1_ref, l2_ref, l3_ref,
                 wa_ref, wb_ref, wo_ref, gm_ref, wu_ref, wd_ref, gf_ref, y_ref,
                 so2_ref, so3_ref, sl2_ref, sl3_ref, *, tm):
    outs = [_tokens(ob1_ref, None, tm), _tokens(ob2_ref, so2_ref, tm), _tokens(ob3_ref, so3_ref, tm)]
    lses = [_tokens(l1_ref, None, tm), _tokens(l2_ref, sl2_ref, tm), _tokens(l3_ref, sl3_ref, tm)]
    top = jnp.maximum(jnp.maximum(lses[0], lses[1]), lses[2])
    ws = [jnp.exp(l - top) for l in lses]
    mix = ws[0] * outs[0] + ws[1] * outs[1] + ws[2] * outs[2]
    o_b = (mix / (ws[0] + ws[1] + ws[2])).astype(_BF16)

    def mixer(rows):
        pa = jnp.dot(oa_ref[rows, :], wa_ref[...], preferred_element_type=_F32)
        pb = jnp.dot(o_b[rows], wb_ref[...], preferred_element_type=_F32)
        merged = (gate_ref[rows, :D_MODEL].astype(_F32) * pa
                  + gate_ref[rows, D_MODEL:].astype(_F32) * pb).astype(_BF16)
        x1 = x_ref[rows, :] + jnp.dot(merged, wo_ref[...], preferred_element_type=_F32)
        return x1, _rms(x1, gm_ref[...]).astype(_BF16)

    def mlp(rows, x1, h):
        acc = x1
        for c in range(D_FF // D_MODEL):
            cols = slice(c * D_MODEL, (c + 1) * D_MODEL)
            u = jnp.maximum(jnp.dot(h, wu_ref[:, cols], preferred_element_type=_F32), 0.0)
            acc = acc + jnp.dot((u * u).astype(_BF16), wd_ref[cols, :], preferred_element_type=_F32)
        y_ref[rows, :] = _rms(acc, gf_ref[...])

    halves = [slice(i * (tm // 2), (i + 1) * (tm // 2)) for i in range(2)]
    mixed = [mixer(rows) for rows in halves]
    for rows, (x1, h) in zip(halves, mixed):
        mlp(rows, x1, h)


def _tail(x, gates, o_a, o_bs, lses, wa, wb, wo, g_mlp, wu, wd, g_final, tm):
    b, s, _ = x.shape
    const = lambda bi, i: (0, 0)
    row = lambda bi, i: (bi, i, 0)
    resident = lambda a: pl.BlockSpec(a.shape, const, pipeline_mode=pl.Buffered(1))
    res_specs = [pl.BlockSpec((None, dil, tm // dil, QUAD), lambda bi, i: (bi, 0, i, 0))
                 for _, dil in B_GROUPS]
    in_specs = (
        [pl.BlockSpec((None, tm, D_MODEL), row), pl.BlockSpec((None, tm, 2 * D_MODEL), row),
         pl.BlockSpec((None, tm, A_Q), row)]
        + res_specs + res_specs
        + [resident(wa), resident(wb), resident(wo), resident(g_mlp), resident(wu), resident(wd),
           resident(g_final)]
    )
    stage = pltpu.VMEM((QUAD // LANES, tm, LANES), _F32)
    return pl.pallas_call(
        functools.partial(_tail_kernel, tm=tm),
        grid=(b, s // tm),
        in_specs=in_specs,
        out_specs=pl.BlockSpec((None, tm, D_MODEL), row),
        out_shape=jax.ShapeDtypeStruct((b, s, D_MODEL), _F32),
        scratch_shapes=[stage] * 4,
        compiler_params=pltpu.CompilerParams(
            dimension_semantics=("arbitrary", "arbitrary"), vmem_limit_bytes=VMEM_LIMIT_BYTES),
        name="tail",
    )(x, gates, o_a, *o_bs, *lses, wa, wb, wo, g_mlp, wu, wd, g_final)


def _prepare(rel_bias, g_mix, w_in, b_gate, w_branch_a, w_branch_b, w_out, attn_sink, g_mlp,
             w_up, w_down, g_final):
    scale = HEAD_DIM ** -0.5 * LOG2E
    order = np.asarray(A_HEAD_ORDER)
    o1, o2, o3 = A_Q, A_Q + A_KV, A_Q + 2 * A_KV
    o4, o5 = o3 + B_W, o3 + 2 * B_W
    o6 = o3 + 3 * B_W
    qa = w_in[:, :o1].reshape(D_MODEL, A_HEADS, HEAD_DIM)[:, order].reshape(D_MODEL, A_Q) * scale
    slabs = [qa, w_in[:, o1:o2], w_in[:, o2:o3]]
    for gi in range(len(B_GROUPS)):
        cols = slice(gi * B_OUT, (gi + 1) * B_OUT)
        slabs += [w_in[:, o3:o4][:, cols] * scale, w_in[:, o4:o5][:, cols], w_in[:, o5:o6][:, cols]]
    w_cat = jnp.concatenate(slabs + [w_in[:, o6:]], axis=1).astype(_BF16)

    wa = w_branch_a.reshape(A_HEADS, HEAD_DIM, D_MODEL)[order].reshape(A_Q, D_MODEL).astype(_BF16)
    bias_a = _bias_blocks(rel_bias[:, :A_HEADS][:, order], A_WINDOW, 1, Q_BLOCK + 2 * A_WINDOW)
    bias_b = []
    for gi, (window, dil) in enumerate(B_GROUPS):
        n = window // (2 * dil)
        table = rel_bias[:, A_HEADS + gi * B_HEADS_PER_GROUP:A_HEADS + (gi + 1) * B_HEADS_PER_GROUP]
        bias_b.append(_bias_blocks(table, n, dil, Q_BLOCK + 2 * n))
    return dict(
        w_cat=w_cat, g_mix=g_mix.reshape(1, D_MODEL), b_gate=b_gate.reshape(1, 2 * D_MODEL),
        wa=wa, wb=w_branch_b.astype(_BF16), wo=w_out.astype(_BF16),
        sink=attn_sink[order].astype(_F32) * LOG2E, g_mlp=g_mlp.reshape(1, D_MODEL),
        wu=w_up.astype(_BF16), wd=w_down.astype(_BF16), g_final=g_final.reshape(1, D_MODEL),
        bias_a=bias_a, bias_b=bias_b,
    )


def _encode(x, p):
    qa, ka, va, zb1, zb2, zb3, gates = _inproj(x, p["g_mix"], p["w_cat"], p["b_gate"], tm=1024)
    o_a = _attn_a(qa, ka, va, p["bias_a"], p["sink"])
    o_bs, lses = [], []
    for gi, ((window, _), zr) in enumerate(zip(B_GROUPS, (zb1, zb2, zb3))):
        o, lse = _attn_b(zr, p["bias_b"][gi], window, f"attn_b{gi}")
        o_bs.append(o)
        lses.append(lse)
    return _tail(x, gates, o_a, o_bs, lses, p["wa"], p["wb"], p["wo"], p["g_mlp"], p["wu"], p["wd"],
                 p["g_final"], tm=512)


def kernel(x_prompt, x_sample, rel_bias, g_mix, w_in, b_gate, w_branch_a, w_branch_b, w_out,
           attn_sink, g_mlp, w_up, w_down, g_final):
    assert g_mix.shape[0] == 1, "single-layer encoder"
    p = _prepare(rel_bias, g_mix[0], w_in[0], b_gate[0], w_branch_a[0], w_branch_b[0], w_out[0],
                 attn_sink[0], g_mlp[0], w_up[0], w_down[0], g_final)
    return (_encode(x_prompt, p), _encode(x_sample, p))
```

```python
import functools
import math

import numpy as np
import jax
import jax.numpy as jnp
from jax import lax
from jax.experimental import pallas as pl
from jax.experimental.pallas import tpu as pltpu

D_MODEL = 1024
HEAD_DIM = 64
A_HEADS = 8
A_KV_HEADS = 2
A_WINDOW = 128
B_GROUPS = ((128, 1), (512, 4), (2048, 16))
B_HEADS_PER_GROUP = 4
B_HEADS = B_HEADS_PER_GROUP * len(B_GROUPS)
REL_BUCKETS = 32
REL_MAX_DISTANCE = 1024
D_FF = 4 * D_MODEL
A_Q = A_HEADS * HEAD_DIM
A_KV = A_KV_HEADS * HEAD_DIM
B_W = B_HEADS * HEAD_DIM
B_OUT = B_HEADS_PER_GROUP * HEAD_DIM
RMS_EPS = 1e-6
NEG_INF = -1e30
LOG2E = math.log2(math.e)
LN2 = math.log(2.0)

LANES = 128
QUAD = 4 * HEAD_DIM
PAIR = 2 * HEAD_DIM
SLAB = 3 * QUAD
Q_BLOCK = 128
SUB_ROWS = 256
VMEM_LIMIT_BYTES = 58 * 1024 * 1024

A_HEAD_ORDER = (0, 4, 1, 5, 2, 6, 3, 7)

_BF16 = jnp.bfloat16
_F32 = jnp.float32


def _rel_bucket(rel):
    half = REL_BUCKETS // 2
    max_exact = half // 2
    n = np.abs(rel)
    large = max_exact + (np.log(np.maximum(n, 1) / max_exact)
                         / np.log(REL_MAX_DISTANCE / max_exact) * (half - max_exact)).astype(np.int32)
    large = np.minimum(large, half - 1)
    return (np.where(rel > 0, half, 0) + np.where(n < max_exact, n, large)).astype(np.int32)


def _bias_blocks(table, n, dist_scale, nk):
    rel = np.arange(nk)[:, None] - n - np.arange(Q_BLOCK)[None, :]
    bucket = _rel_bucket(rel * dist_scale)
    table = table.astype(_F32) * LOG2E
    bias = jnp.zeros((table.shape[1], nk, Q_BLOCK), _F32)
    for bkt in np.unique(bucket):
        bias = jnp.where(jnp.asarray(bucket == bkt)[None], table[bkt][:, None, None], bias)
    band = np.abs(rel) <= n
    key = np.arange(nk)[:, None]
    variants = []
    for var in range(4):
        ok = band.copy()
        if var & 1:
            ok &= key >= n
        if var & 2:
            ok &= key < nk - n
        variants.append(jnp.where(ok[None], bias, NEG_INF))
    tiles = jnp.stack(variants)
    if _folds(n):
        tiles = jnp.concatenate(
            [jnp.where(np.arange(Q_BLOCK) < n, tiles[..., :n, :], tiles[..., nk - n:, :]),
             tiles[..., n:nk - n, :]], axis=-2)
    return tiles


def _folds(halo):
    return 2 * halo == Q_BLOCK


def _inproj_kernel(x_ref, g_ref, w_ref, b_ref, qa_ref, ka_ref, va_ref, zb1_ref, zb2_ref, zb3_ref,
                   gate_ref, zs_ref, *, tm):
    x = x_ref[...]
    ms = jnp.mean(x * x, axis=-1, keepdims=True)
    h = (x * lax.rsqrt(ms + RMS_EPS) * g_ref[...]).astype(_BF16)
    for sub in range(tm // SUB_ROWS):
        rows = slice(sub * SUB_ROWS, (sub + 1) * SUB_ROWS)
        hs = h[rows]
        zg = jnp.dot(hs, w_ref[:, 4 * SLAB:], preferred_element_type=_F32) + b_ref[...]
        gate_ref[rows, :] = (0.5 * jnp.tanh(0.5 * zg) + 0.5).astype(_BF16)
        for gi in reversed(range(len(B_GROUPS))):
            out, dil = (zb1_ref, zb2_ref, zb3_ref)[gi], B_GROUPS[gi][1]
            z = jnp.dot(hs, w_ref[:, (gi + 1) * SLAB:(gi + 2) * SLAB], preferred_element_type=_F32)
            if dil == 1:
                for part in range(3):
                    out[0, part, rows, :] = z[:, part * QUAD:(part + 1) * QUAD].astype(_BF16)
                continue
            for k in range(SLAB // LANES):
                zs_ref[k, rows, :] = z[:, k * LANES:(k + 1) * LANES]
            per = SUB_ROWS // dil
            for r in range(dil):
                for k in range(SLAB // LANES):
                    part, half = divmod(k, QUAD // LANES)
                    out[r, part, sub * per:(sub + 1) * per, half * LANES:(half + 1) * LANES] = (
                        zs_ref[k, pl.ds(sub * SUB_ROWS + r, per, stride=dil), :].astype(_BF16))
        za = jnp.dot(hs, w_ref[:, :SLAB], preferred_element_type=_F32).astype(_BF16)
        qa_ref[rows, :] = za[:, :A_Q]
        ka_ref[rows, :] = za[:, A_Q:A_Q + A_KV]
        va_ref[rows, :] = za[:, A_Q + A_KV:]


def _inproj(x, g_mix, w_cat, b_gate, tm):
    b, s, _ = x.shape
    const = lambda bi, i: (0, 0)
    row = lambda bi, i: (bi, i, 0)
    res_specs = [pl.BlockSpec((None, dil, 3, tm // dil, QUAD), lambda bi, i: (bi, 0, 0, i, 0))
                 for _, dil in B_GROUPS]
    res_shapes = [jax.ShapeDtypeStruct((b, dil, 3, s // dil, QUAD), _BF16) for _, dil in B_GROUPS]
    a_widths = (A_Q, A_KV, A_KV)
    return pl.pallas_call(
        functools.partial(_inproj_kernel, tm=tm),
        grid=(b, s // tm),
        in_specs=[
            pl.BlockSpec((None, tm, D_MODEL), row),
            pl.BlockSpec((1, D_MODEL), const),
            pl.BlockSpec(w_cat.shape, const, pipeline_mode=pl.Buffered(1)),
            pl.BlockSpec((1, 2 * D_MODEL), const),
        ],
        out_specs=[pl.BlockSpec((None, tm, w), row) for w in a_widths] + res_specs
        + [pl.BlockSpec((None, tm, 2 * D_MODEL), row)],
        out_shape=[jax.ShapeDtypeStruct((b, s, w), _BF16) for w in a_widths] + res_shapes
        + [jax.ShapeDtypeStruct((b, s, 2 * D_MODEL), _BF16)],
        scratch_shapes=[pltpu.VMEM((SLAB // LANES, tm, LANES), _F32)],
        compiler_params=pltpu.CompilerParams(
            dimension_semantics=("arbitrary", "arbitrary"), vmem_limit_bytes=VMEM_LIMIT_BYTES),
        name="inproj",
    )(x, g_mix, w_cat, b_gate)


def _attn_ops(q_ref, kv_refs, bias_ref, sink_ref, stage_refs, write, *, halo, lc, kv_pairs,
              first, last, want_lse):
    kc_ref, vc_ref, kp_ref, vp_ref, kn_ref, vn_ref = kv_refs
    kext_ref, vext_ref, vt_ref = stage_refs
    nk = Q_BLOCK + 2 * halo
    nblk = lc // Q_BLOCK
    qgrp = lax.broadcasted_iota(jnp.int32, (Q_BLOCK, QUAD), 1) // HEAD_DIM
    fold = _folds(halo)
    low_queries = lax.broadcasted_iota(jnp.int32, (halo, Q_BLOCK), 1) < halo

    def stage(r):
        pieces = ((0, halo, kp_ref, vp_ref), (halo, lc, kc_ref, vc_ref),
                  (halo + lc, halo, kn_ref, vn_ref))
        for lo, rows, k_src, v_src in pieces:
            k = k_src[r]
            kext_ref[r, lo:lo + rows, :] = k if kv_pairs == 2 else jnp.concatenate([k, k], axis=-1)
            vext_ref[r, lo:lo + rows, :] = v_src[r]
        vt_ref[r] = vext_ref[r].astype(_F32).T.astype(_BF16)

    def scores(r, i, quad):
        if i == 0 and quad == 0:
            stage(r)
        r0 = i * Q_BLOCK
        q4 = q_ref[r, r0:r0 + Q_BLOCK, quad * QUAD:(quad + 1) * QUAD]
        qs = jnp.concatenate(
            [jnp.where(qgrp == j, q4, jnp.zeros_like(q4)) for j in range(4)], axis=0)
        return lax.dot_general(kext_ref[r, r0:r0 + nk, :], qs, (((1,), (1,)), ((), ())),
                               preferred_element_type=_F32)

    def finish(r, i, quad, st):
        r0 = i * Q_BLOCK
        var = (first if i == 0 else 0) + (last if i == nblk - 1 else 0)
        pts, scales, lses = [], [], []
        for j in range(4):
            head = quad * 4 + j
            s = st[:, j * Q_BLOCK:(j + 1) * Q_BLOCK]
            if fold:
                s = jnp.concatenate(
                    [jnp.where(low_queries, s[:halo], s[nk - halo:]), s[halo:nk - halo]], axis=0)
            logits = s + bias_ref[var, head]
            m = jnp.max(logits, axis=0, keepdims=True)
            if sink_ref is not None:
                sink = sink_ref[head]
                m = jnp.maximum(m, sink)
            p = jnp.exp2(logits - m)
            den = jnp.sum(p, axis=0, keepdims=True)
            if sink_ref is not None:
                den = den + jnp.exp2(sink - m)
            if fold:
                shared, zero = p[:halo], jnp.zeros((halo, Q_BLOCK), _F32)
                p = jnp.concatenate([jnp.where(low_queries, shared, zero), p[halo:],
                                     jnp.where(low_queries, zero, shared)], axis=0)
            pts.append(p.astype(_BF16))
            scales.append(1.0 / den)
            lses.append(m * LN2 + jnp.log(den))
        outs = []
        for pair in range(2):
            j0 = 2 * pair
            vrows = (pair % kv_pairs) * PAIR
            ot = jnp.dot(vt_ref[r, vrows:vrows + PAIR, r0:r0 + nk],
                         jnp.concatenate(pts[j0:j0 + 2], axis=1), preferred_element_type=_F32)
            outs.append(ot[:HEAD_DIM, :Q_BLOCK] * scales[j0])
            outs.append(ot[HEAD_DIM:, Q_BLOCK:] * scales[j0 + 1])
        lse = None
        if want_lse:
            lse = jnp.concatenate(
                [jnp.broadcast_to(l, (HEAD_DIM, Q_BLOCK)) for l in lses], axis=0).T
        write(r, r0, quad, jnp.concatenate(outs, axis=0).T.astype(_BF16), lse)

    return scores, finish


def _attn_kernel(*refs, halo, lc, nres, nquad, kv_pairs, has_sink, want_lse, chunk_axis):
    it = iter(refs)
    q_ref = next(it)
    kv_refs = [next(it) for _ in range(6)]
    bias_ref = next(it)
    sink_ref = next(it) if has_sink else None
    o_ref = next(it)
    lse_ref = next(it) if want_lse else None
    stage_refs = [next(it) for _ in range(3)]

    c = pl.program_id(chunk_axis)
    first = jnp.where(c == 0, 1, 0)
    last = jnp.where(c == pl.num_programs(chunk_axis) - 1, 2, 0)

    def write(r, r0, quad, o, lse):
        o_ref[r, r0:r0 + Q_BLOCK, quad * QUAD:(quad + 1) * QUAD] = o
        if want_lse:
            lse_ref[r, r0:r0 + Q_BLOCK, :] = lse

    scores, finish = _attn_ops(q_ref, kv_refs, bias_ref, sink_ref, stage_refs, write, halo=halo,
                               lc=lc, kv_pairs=kv_pairs, first=first, last=last, want_lse=want_lse)

    units = [(r, i, quad) for r in range(nres) for i in range(lc // Q_BLOCK) for quad in range(nquad)]
    ahead = 2 if nquad == 1 else 1
    pending = [scores(*u) for u in units[:ahead]]
    for idx, unit in enumerate(units):
        if idx + ahead < len(units):
            pending.append(scores(*units[idx + ahead]))
        finish(*unit, pending.pop(0))


UNITS_PER_STEP = 32


def _attn_call(q, k, v, bias, sink, *, halo, lc, nres, want_lse, name):
    def operand(x):
        arr, part = x if isinstance(x, tuple) else (x, None)
        width = arr.shape[-1]

        def spec(rows, row_block):
            if part is None:
                return pl.BlockSpec((None, nres, rows, width),
                                    lambda bi, r, c: (bi, r, row_block(c), 0))
            return pl.BlockSpec((None, nres, None, rows, width),
                                lambda bi, r, c: (bi, r, part, row_block(c), 0))
        return arr, spec

    (q_arr, q_spec), (k_arr, k_spec), (v_arr, v_spec) = operand(q), operand(k), operand(v)
    b, dil, length, qw, kvw = q_arr.shape[0], q_arr.shape[1], q_arr.shape[-2], q_arr.shape[-1], k_arr.shape[-1]
    per_chunk = lc // halo
    ext = lc + 2 * halo
    here = lambda c: c
    prev = lambda c: jnp.maximum(c * per_chunk - 1, 0)
    nxt = lambda c: jnp.minimum((c + 1) * per_chunk, length // halo - 1)
    in_specs = [
        q_spec(lc, here), k_spec(lc, here), v_spec(lc, here),
        k_spec(halo, prev), v_spec(halo, prev), k_spec(halo, nxt), v_spec(halo, nxt),
        pl.BlockSpec(bias.shape, lambda bi, r, c: (0, 0, 0, 0), pipeline_mode=pl.Buffered(1)),
    ]
    args = [q_arr, k_arr, v_arr, k_arr, v_arr, k_arr, v_arr, bias]
    if sink is not None:
        in_specs.append(pl.BlockSpec(memory_space=pltpu.SMEM))
        args.append(sink)
    out_spec = pl.BlockSpec((None, nres, lc, qw), lambda bi, r, c: (bi, r, c, 0))
    out_specs, out_shape = [out_spec], [jax.ShapeDtypeStruct((b, dil, length, qw), _BF16)]
    if want_lse:
        out_specs.append(out_spec)
        out_shape.append(jax.ShapeDtypeStruct((b, dil, length, qw), _F32))
    kernel = functools.partial(
        _attn_kernel, halo=halo, lc=lc, nres=nres, nquad=qw // QUAD, kv_pairs=kvw // PAIR,
        has_sink=sink is not None, want_lse=want_lse, chunk_axis=2)
    grid = (b, dil // nres, length // lc)
    return pl.pallas_call(
        kernel,
        grid=grid,
        in_specs=in_specs,
        out_specs=out_specs,
        out_shape=out_shape,
        scratch_shapes=[
            pltpu.VMEM((nres, ext, QUAD), _BF16),
            pltpu.VMEM((nres, ext, kvw), _BF16),
            pltpu.VMEM((nres, kvw, ext), _BF16),
        ],
        compiler_params=pltpu.CompilerParams(
            dimension_semantics=("arbitrary",) * len(grid), vmem_limit_bytes=VMEM_LIMIT_BYTES),
        name=name,
    )(*args)


def _attn_a(qa, ka, va, bias, sink):
    b, s, _ = qa.shape
    lc = min(s, Q_BLOCK * UNITS_PER_STEP // (A_Q // QUAD))
    seq = lambda a: a.reshape(b, 1, s, a.shape[-1])
    (o,) = _attn_call(seq(qa), seq(ka), seq(va), bias, sink, halo=A_WINDOW, lc=lc, nres=1,
                      want_lse=False, name="attn_a")
    return o.reshape(b, s, A_Q)


def _attn_b(zr, bias, window, name):
    _, dil, _, length, _ = zr.shape
    lc = min(length, 1024 if dil > 1 else Q_BLOCK * UNITS_PER_STEP)
    nres = min(dil, UNITS_PER_STEP // (lc // Q_BLOCK))
    return _attn_call((zr, 0), (zr, 1), (zr, 2), bias, None, halo=window // (2 * dil), lc=lc,
                      nres=nres, want_lse=True, name=name)


def _rms(x, g):
    return x * lax.rsqrt(jnp.mean(x * x, axis=-1, keepdims=True) + RMS_EPS) * g


def _tokens(src_ref, stage_ref, tm):
    dil = src_ref.shape[0]
    if dil == 1:
        return src_ref[0].astype(_F32)
    for r in range(dil):
        for k in range(QUAD // LANES):
            stage_ref[k, pl.ds(r, tm // dil, stride=dil), :] = (
                src_ref[r, :, k * LANES:(k + 1) * LANES].astype(_F32))
    return jnp.concatenate([stage_ref[k] for k in range(QUAD // LANES)], axis=-1)


def _tail_kernel(x_ref, gate_ref, oa_ref, ob1_ref, ob2_ref, ob3_ref, l1_ref, l2_ref, l3_ref,
                 wa_ref, wb_ref, wo_ref, gm_ref, wu_ref, wd_ref, gf_ref, y_ref,
                 so2_ref, so3_ref, sl2_ref, sl3_ref, *, tm):
    outs = [_tokens(ob1_ref, None, tm), _tokens(ob2_ref, so2_ref, tm), _tokens(ob3_ref, so3_ref, tm)]
    lses = [_tokens(l1_ref, None, tm), _tokens(l2_ref, sl2_ref, tm), _tokens(l3_ref, sl3_ref, tm)]
    top = jnp.maximum(jnp.maximum(lses[0], lses[1]), lses[2])
    ws = [jnp.exp(l - top) for l in lses]
    mix = ws[0] * outs[0] + ws[1] * outs[1] + ws[2] * outs[2]
    o_b = (mix / (ws[0] + ws[1] + ws[2])).astype(_BF16)

    def mixer(rows):
        pa = jnp.dot(oa_ref[rows, :], wa_ref[...], preferred_element_type=_F32)
        pb = jnp.dot(o_b[rows], wb_ref[...], preferred_element_type=_F32)
        merged = (gate_ref[rows, :D_MODEL].astype(_F32) * pa
                  + gate_ref[rows, D_MODEL:].astype(_F32) * pb).astype(_BF16)
        x1 = x_ref[rows, :] + jnp.dot(merged, wo_ref[...], preferred_element_type=_F32)
        return x1, _rms(x1, gm_ref[...]).astype(_BF16)

    def mlp(rows, x1, h):
        acc = x1
        for c in range(D_FF // D_MODEL):
            cols = slice(c * D_MODEL, (c + 1) * D_MODEL)
            u = jnp.maximum(jnp.dot(h, wu_ref[:, cols], preferred_element_type=_F32), 0.0)
            acc = acc + jnp.dot((u * u).astype(_BF16), wd_ref[cols, :], preferred_element_type=_F32)
        y_ref[rows, :] = _rms(acc, gf_ref[...])

    halves = [slice(i * (tm // 2), (i + 1) * (tm // 2)) for i in range(2)]
    mixed = [mixer(rows) for rows in halves]
    for rows, (x1, h) in zip(halves, mixed):
        mlp(rows, x1, h)


def _tail(x, gates, o_a, o_bs, lses, wa, wb, wo, g_mlp, wu, wd, g_final, tm):
    b, s, _ = x.shape
    const = lambda bi, i: (0, 0)
    row = lambda bi, i: (bi, i, 0)
    resident = lambda a: pl.BlockSpec(a.shape, const, pipeline_mode=pl.Buffered(1))
    res_specs = [pl.BlockSpec((None, dil, tm // dil, QUAD), lambda bi, i: (bi, 0, i, 0))
                 for _, dil in B_GROUPS]
    in_specs = (
        [pl.BlockSpec((None, tm, D_MODEL), row), pl.BlockSpec((None, tm, 2 * D_MODEL), row),
         pl.BlockSpec((None, tm, A_Q), row)]
        + res_specs + res_specs
        + [resident(wa), resident(wb), resident(wo), resident(g_mlp), resident(wu), resident(wd),
           resident(g_final)]
    )
    stage = pltpu.VMEM((QUAD // LANES, tm, LANES), _F32)
    return pl.pallas_call(
        functools.partial(_tail_kernel, tm=tm),
        grid=(b, s // tm),
        in_specs=in_specs,
        out_specs=pl.BlockSpec((None, tm, D_MODEL), row),
        out_shape=jax.ShapeDtypeStruct((b, s, D_MODEL), _F32),
        scratch_shapes=[stage] * 4,
        compiler_params=pltpu.CompilerParams(
            dimension_semantics=("arbitrary", "arbitrary"), vmem_limit_bytes=VMEM_LIMIT_BYTES),
        name="tail",
    )(x, gates, o_a, *o_bs, *lses, wa, wb, wo, g_mlp, wu, wd, g_final)


def _prepare(rel_bias, g_mix, w_in, b_gate, w_branch_a, w_branch_b, w_out, attn_sink, g_mlp,
             w_up, w_down, g_final):
    scale = HEAD_DIM ** -0.5 * LOG2E
    order = np.asarray(A_HEAD_ORDER)
    o1, o2, o3 = A_Q, A_Q + A_KV, A_Q + 2 * A_KV
    o4, o5 = o3 + B_W, o3 + 2 * B_W
    o6 = o3 + 3 * B_W
    qa = w_in[:, :o1].reshape(D_MODEL, A_HEADS, HEAD_DIM)[:, order].reshape(D_MODEL, A_Q) * scale
    slabs = [qa, w_in[:, o1:o2], w_in[:, o2:o3]]
    for gi in range(len(B_GROUPS)):
        cols = slice(gi * B_OUT, (gi + 1) * B_OUT)
        slabs += [w_in[:, o3:o4][:, cols] * scale, w_in[:, o4:o5][:, cols], w_in[:, o5:o6][:, cols]]
    w_cat = jnp.concatenate(slabs + [w_in[:, o6:]], axis=1).astype(_BF16)

    wa = w_branch_a.reshape(A_HEADS, HEAD_DIM, D_MODEL)[order].reshape(A_Q, D_MODEL).astype(_BF16)
    bias_a = _bias_blocks(rel_bias[:, :A_HEADS][:, order], A_WINDOW, 1, Q_BLOCK + 2 * A_WINDOW)
    bias_b = []
    for gi, (window, dil) in enumerate(B_GROUPS):
        n = window // (2 * dil)
        table = rel_bias[:, A_HEADS + gi * B_HEADS_PER_GROUP:A_HEADS + (gi + 1) * B_HEADS_PER_GROUP]
        bias_b.append(_bias_blocks(table, n, dil, Q_BLOCK + 2 * n))
    return dict(
        w_cat=w_cat, g_mix=g_mix.reshape(1, D_MODEL), b_gate=b_gate.reshape(1, 2 * D_MODEL),
        wa=wa, wb=w_branch_b.astype(_BF16), wo=w_out.astype(_BF16),
        sink=attn_sink[order].astype(_F32) * LOG2E, g_mlp=g_mlp.reshape(1, D_MODEL),
        wu=w_up.astype(_BF16), wd=w_down.astype(_BF16), g_final=g_final.reshape(1, D_MODEL),
        bias_a=bias_a, bias_b=bias_b,
    )


def _encode(x, p):
    qa, ka, va, zb1, zb2, zb3, gates = _inproj(x, p["g_mix"], p["w_cat"], p["b_gate"], tm=1024)
    o_a = _attn_a(qa, ka, va, p["bias_a"], p["sink"])
    o_bs, lses = [], []
    for gi, ((window, _), zr) in enumerate(zip(B_GROUPS, (zb1, zb2, zb3))):
        o, lse = _attn_b(zr, p["bias_b"][gi], window, f"attn_b{gi}")
        o_bs.append(o)
        lses.append(lse)
    return _tail(x, gates, o_a, o_bs, lses, p["wa"], p["wb"], p["wo"], p["g_mlp"], p["wu"], p["wd"],
                 p["g_final"], tm=512)


def kernel(x_prompt, x_sample, rel_bias, g_mix, w_in, b_gate, w_branch_a, w_branch_b, w_out,
           attn_sink, g_mlp, w_up, w_down, g_final):
    assert g_mix.shape[0] == 1, "single-layer encoder"
    p = _prepare(rel_bias, g_mix[0], w_in[0], b_gate[0], w_branch_a[0], w_branch_b[0], w_out[0],
                 attn_sink[0], g_mlp[0], w_up[0], w_down[0], g_final)
    return (_encode(x_prompt, p), _encode(x_sample, p))
```

```python
import functools
import math

import numpy as np
import jax
import jax.numpy as jnp
from jax import lax
from jax.experimental import pallas as pl
from jax.experimental.pallas import tpu as pltpu

D_MODEL = 1024
HEAD_DIM = 64
A_HEADS = 8
A_KV_HEADS = 2
A_WINDOW = 128
B_GROUPS = ((128, 1), (512, 4), (2048, 16))
B_HEADS_PER_GROUP = 4
B_HEADS = B_HEADS_PER_GROUP * len(B_GROUPS)
REL_BUCKETS = 32
REL_MAX_DISTANCE = 1024
D_FF = 4 * D_MODEL
A_Q = A_HEADS * HEAD_DIM
A_KV = A_KV_HEADS * HEAD_DIM
B_W = B_HEADS * HEAD_DIM
B_OUT = B_HEADS_PER_GROUP * HEAD_DIM
RMS_EPS = 1e-6
NEG_INF = -1e30
LOG2E = math.log2(math.e)
LN2 = math.log(2.0)

LANES = 128
QUAD = 4 * HEAD_DIM
PAIR = 2 * HEAD_DIM
SLAB = 3 * QUAD
Q_BLOCK = 128
FOLD = Q_BLOCK // 2
SUB_ROWS = 256
VMEM_LIMIT_BYTES = 58 * 1024 * 1024

A_HEAD_ORDER = (0, 4, 1, 5, 2, 6, 3, 7)

_BF16 = jnp.bfloat16
_F32 = jnp.float32


def _rel_bucket(rel):
    half = REL_BUCKETS // 2
    max_exact = half // 2
    n = np.abs(rel)
    large = max_exact + (np.log(np.maximum(n, 1) / max_exact)
                         / np.log(REL_MAX_DISTANCE / max_exact) * (half - max_exact)).astype(np.int32)
    large = np.minimum(large, half - 1)
    return (np.where(rel > 0, half, 0) + np.where(n < max_exact, n, large)).astype(np.int32)


def _bias_blocks(table, n, dist_scale, nk):
    rel = np.arange(nk)[:, None] - n - np.arange(Q_BLOCK)[None, :]
    bucket = _rel_bucket(rel * dist_scale)
    table = table.astype(_F32) * LOG2E
    bias = jnp.zeros((table.shape[1], nk, Q_BLOCK), _F32)
    for bkt in np.unique(bucket):
        bias = jnp.where(jnp.asarray(bucket == bkt)[None], table[bkt][:, None, None], bias)
    band = np.abs(rel) <= n
    key = np.arange(nk)[:, None]
    variants = []
    for var in range(4):
        ok = band.copy()
        if var & 1:
            ok &= key >= n
        if var & 2:
            ok &= key < nk - n
        variants.append(jnp.where(ok[None], bias, NEG_INF))
    tiles = jnp.stack(variants)
    return jnp.concatenate(
        [jnp.where(np.arange(Q_BLOCK) < FOLD, tiles[..., :FOLD, :], tiles[..., nk - FOLD:, :]),
         tiles[..., FOLD:nk - FOLD, :]], axis=-2)


def _inproj_kernel(x_ref, g_ref, w_ref, b_ref, qa_ref, ka_ref, va_ref, zb1_ref, zb2_ref, zb3_ref,
                   gate_ref, zs_ref, *, tm):
    x = x_ref[...]
    ms = jnp.mean(x * x, axis=-1, keepdims=True)
    h = (x * lax.rsqrt(ms + RMS_EPS) * g_ref[...]).astype(_BF16)
    for sub in range(tm // SUB_ROWS):
        rows = slice(sub * SUB_ROWS, (sub + 1) * SUB_ROWS)
        hs = h[rows]
        zg = jnp.dot(hs, w_ref[:, 4 * SLAB:], preferred_element_type=_F32) + b_ref[...]
        gate_ref[rows, :] = (0.5 * jnp.tanh(0.5 * zg) + 0.5).astype(_BF16)
        for gi in reversed(range(len(B_GROUPS))):
            out, dil = (zb1_ref, zb2_ref, zb3_ref)[gi], B_GROUPS[gi][1]
            z = jnp.dot(hs, w_ref[:, (gi + 1) * SLAB:(gi + 2) * SLAB], preferred_element_type=_F32)
            if dil == 1:
                for part in range(3):
                    out[0, part, rows, :] = z[:, part * QUAD:(part + 1) * QUAD].astype(_BF16)
                continue
            for k in range(SLAB // LANES):
                zs_ref[k, rows, :] = z[:, k * LANES:(k + 1) * LANES]
            per = SUB_ROWS // dil
            for r in range(dil):
                for k in range(SLAB // LANES):
                    part, half = divmod(k, QUAD // LANES)
                    out[r, part, sub * per:(sub + 1) * per, half * LANES:(half + 1) * LANES] = (
                        zs_ref[k, pl.ds(sub * SUB_ROWS + r, per, stride=dil), :].astype(_BF16))
        za = jnp.dot(hs, w_ref[:, :SLAB], preferred_element_type=_F32).astype(_BF16)
        qa_ref[rows, :] = za[:, :A_Q]
        ka_ref[rows, :] = za[:, A_Q:A_Q + A_KV]
        va_ref[rows, :] = za[:, A_Q + A_KV:]


def _inproj(x, g_mix, w_cat, b_gate, tm):
    b, s, _ = x.shape
    const = lambda bi, i: (0, 0)
    row = lambda bi, i: (bi, i, 0)
    res_specs = [pl.BlockSpec((None, dil, 3, tm // dil, QUAD), lambda bi, i: (bi, 0, 0, i, 0))
                 for _, dil in B_GROUPS]
    res_shapes = [jax.ShapeDtypeStruct((b, dil, 3, s // dil, QUAD), _BF16) for _, dil in B_GROUPS]
    a_widths = (A_Q, A_KV, A_KV)
    return pl.pallas_call(
        functools.partial(_inproj_kernel, tm=tm),
        grid=(b, s // tm),
        in_specs=[
            pl.BlockSpec((None, tm, D_MODEL), row),
            pl.BlockSpec((1, D_MODEL), const),
            pl.BlockSpec(w_cat.shape, const, pipeline_mode=pl.Buffered(1)),
            pl.BlockSpec((1, 2 * D_MODEL), const),
        ],
        out_specs=[pl.BlockSpec((None, tm, w), row) for w in a_widths] + res_specs
        + [pl.BlockSpec((None, tm, 2 * D_MODEL), row)],
        out_shape=[jax.ShapeDtypeStruct((b, s, w), _BF16) for w in a_widths] + res_shapes
        + [jax.ShapeDtypeStruct((b, s, 2 * D_MODEL), _BF16)],
        scratch_shapes=[pltpu.VMEM((SLAB // LANES, tm, LANES), _F32)],
        compiler_params=pltpu.CompilerParams(
            dimension_semantics=("arbitrary", "arbitrary"), vmem_limit_bytes=VMEM_LIMIT_BYTES),
        name="inproj",
    )(x, g_mix, w_cat, b_gate)


def _attn_ops(q_ref, kv_refs, bias_ref, sink_ref, stage_refs, write, *, halo, lc, kv_pairs,
              first, last, want_lse):
    kc_ref, vc_ref, kp_ref, vp_ref, kn_ref, vn_ref = kv_refs
    kext_ref, vext_ref, vt_ref = stage_refs
    nk = Q_BLOCK + 2 * halo
    nblk = lc // Q_BLOCK
    qgrp = lax.broadcasted_iota(jnp.int32, (Q_BLOCK, QUAD), 1) // HEAD_DIM
    low_queries = lax.broadcasted_iota(jnp.int32, (FOLD, Q_BLOCK), 1) < FOLD

    def stage(r):
        pieces = ((0, halo, kp_ref, vp_ref), (halo, lc, kc_ref, vc_ref),
                  (halo + lc, halo, kn_ref, vn_ref))
        for lo, rows, k_src, v_src in pieces:
            k = k_src[r]
            kext_ref[r, lo:lo + rows, :] = k if kv_pairs == 2 else jnp.concatenate([k, k], axis=-1)
            vext_ref[r, lo:lo + rows, :] = v_src[r]
        vt_ref[r] = vext_ref[r].astype(_F32).T.astype(_BF16)

    def scores(r, i, quad):
        if i == 0 and quad == 0:
            stage(r)
        r0 = i * Q_BLOCK
        q4 = q_ref[r, r0:r0 + Q_BLOCK, quad * QUAD:(quad + 1) * QUAD]
        qs = jnp.concatenate(
            [jnp.where(qgrp == j, q4, jnp.zeros_like(q4)) for j in range(4)], axis=0)
        return lax.dot_general(kext_ref[r, r0:r0 + nk, :], qs, (((1,), (1,)), ((), ())),
                               preferred_element_type=_F32)

    def finish(r, i, quad, st):
        r0 = i * Q_BLOCK
        var = (first if i == 0 else 0) + (last if i == nblk - 1 else 0)
        pts, scales, lses = [], [], []
        for j in range(4):
            head = quad * 4 + j
            s = st[:, j * Q_BLOCK:(j + 1) * Q_BLOCK]
            s = jnp.concatenate(
                [jnp.where(low_queries, s[:FOLD], s[nk - FOLD:]), s[FOLD:nk - FOLD]], axis=0)
            logits = s + bias_ref[var, head]
            m = jnp.max(logits, axis=0, keepdims=True)
            if sink_ref is not None:
                sink = sink_ref[head]
                m = jnp.maximum(m, sink)
            p = jnp.exp2(logits - m)
            den = jnp.sum(p, axis=0, keepdims=True)
            if sink_ref is not None:
                den = den + jnp.exp2(sink - m)
            shared, zero = p[:FOLD], jnp.zeros((FOLD, Q_BLOCK), _F32)
            p = jnp.concatenate([jnp.where(low_queries, shared, zero), p[FOLD:],
                                 jnp.where(low_queries, zero, shared)], axis=0)
            pts.append(p.astype(_BF16))
            scales.append(1.0 / den)
            lses.append(m * LN2 + jnp.log(den))
        outs = []
        for pair in range(2):
            j0 = 2 * pair
            vrows = (pair % kv_pairs) * PAIR
            ot = jnp.dot(vt_ref[r, vrows:vrows + PAIR, r0:r0 + nk],
                         jnp.concatenate(pts[j0:j0 + 2], axis=1), preferred_element_type=_F32)
            outs.append(ot[:HEAD_DIM, :Q_BLOCK] * scales[j0])
            outs.append(ot[HEAD_DIM:, Q_BLOCK:] * scales[j0 + 1])
        lse = None
        if want_lse:
            lse = jnp.concatenate(
                [jnp.broadcast_to(l, (HEAD_DIM, Q_BLOCK)) for l in lses], axis=0).T
        write(r, r0, quad, jnp.concatenate(outs, axis=0).T.astype(_BF16), lse)

    return scores, finish


def _attn_kernel(*refs, halo, lc, nres, nquad, kv_pairs, has_sink, want_lse, chunk_axis):
    it = iter(refs)
    q_ref = next(it)
    kv_refs = [next(it) for _ in range(6)]
    bias_ref = next(it)
    sink_ref = next(it) if has_sink else None
    o_ref = next(it)
    lse_ref = next(it) if want_lse else None
    stage_refs = [next(it) for _ in range(3)]

    c = pl.program_id(chunk_axis)
    first = jnp.where(c == 0, 1, 0)
    last = jnp.where(c == pl.num_programs(chunk_axis) - 1, 2, 0)

    def write(r, r0, quad, o, lse):
        o_ref[r, r0:r0 + Q_BLOCK, quad * QUAD:(quad + 1) * QUAD] = o
        if want_lse:
            lse_ref[r, r0:r0 + Q_BLOCK, :] = lse

    scores, finish = _attn_ops(q_ref, kv_refs, bias_ref, sink_ref, stage_refs, write, halo=halo,
                               lc=lc, kv_pairs=kv_pairs, first=first, last=last, want_lse=want_lse)

    units = [(r, i, quad) for r in range(nres) for i in range(lc // Q_BLOCK) for quad in range(nquad)]
    ahead = 2 if nquad == 1 else 1
    pending = [scores(*u) for u in units[:ahead]]
    for idx, unit in enumerate(units):
        if idx + ahead < len(units):
            pending.append(scores(*units[idx + ahead]))
        finish(*unit, pending.pop(0))


UNITS_PER_STEP = 32


def _attn_call(q, k, v, bias, sink, *, halo, lc, nres, want_lse, name):
    def operand(x):
        arr, part = x if isinstance(x, tuple) else (x, None)
        width = arr.shape[-1]

        def spec(rows, row_block):
            if part is None:
                return pl.BlockSpec((None, nres, rows, width),
                                    lambda bi, r, c: (bi, r, row_block(c), 0))
            return pl.BlockSpec((None, nres, None, rows, width),
                                lambda bi, r, c: (bi, r, part, row_block(c), 0))
        return arr, spec

    (q_arr, q_spec), (k_arr, k_spec), (v_arr, v_spec) = operand(q), operand(k), operand(v)
    b, dil, length, qw, kvw = q_arr.shape[0], q_arr.shape[1], q_arr.shape[-2], q_arr.shape[-1], k_arr.shape[-1]
    per_chunk = lc // halo
    ext = lc + 2 * halo
    here = lambda c: c
    prev = lambda c: jnp.maximum(c * per_chunk - 1, 0)
    nxt = lambda c: jnp.minimum((c + 1) * per_chunk, length // halo - 1)
    in_specs = [
        q_spec(lc, here), k_spec(lc, here), v_spec(lc, here),
        k_spec(halo, prev), v_spec(halo, prev), k_spec(halo, nxt), v_spec(halo, nxt),
        pl.BlockSpec(bias.shape, lambda bi, r, c: (0, 0, 0, 0), pipeline_mode=pl.Buffered(1)),
    ]
    args = [q_arr, k_arr, v_arr, k_arr, v_arr, k_arr, v_arr, bias]
    if sink is not None:
        in_specs.append(pl.BlockSpec(memory_space=pltpu.SMEM))
        args.append(sink)
    out_spec = pl.BlockSpec((None, nres, lc, qw), lambda bi, r, c: (bi, r, c, 0))
    out_specs, out_shape = [out_spec], [jax.ShapeDtypeStruct((b, dil, length, qw), _BF16)]
    if want_lse:
        out_specs.append(out_spec)
        out_shape.append(jax.ShapeDtypeStruct((b, dil, length, qw), _F32))
    kernel = functools.partial(
        _attn_kernel, halo=halo, lc=lc, nres=nres, nquad=qw // QUAD, kv_pairs=kvw // PAIR,
        has_sink=sink is not None, want_lse=want_lse, chunk_axis=2)
    grid = (b, dil // nres, length // lc)
    return pl.pallas_call(
        kernel,
        grid=grid,
        in_specs=in_specs,
        out_specs=out_specs,
        out_shape=out_shape,
        scratch_shapes=[
            pltpu.VMEM((nres, ext, QUAD), _BF16),
            pltpu.VMEM((nres, ext, kvw), _BF16),
            pltpu.VMEM((nres, kvw, ext), _BF16),
        ],
        compiler_params=pltpu.CompilerParams(
            dimension_semantics=("arbitrary",) * len(grid), vmem_limit_bytes=VMEM_LIMIT_BYTES),
        name=name,
    )(*args)


def _attn_a(qa, ka, va, bias, sink):
    b, s, _ = qa.shape
    lc = min(s, Q_BLOCK * UNITS_PER_STEP // (A_Q // QUAD))
    seq = lambda a: a.reshape(b, 1, s, a.shape[-1])
    (o,) = _attn_call(seq(qa), seq(ka), seq(va), bias, sink, halo=A_WINDOW, lc=lc, nres=1,
                      want_lse=False, name="attn_a")
    return o.reshape(b, s, A_Q)


def _attn_b(zr, bias, window, name):
    _, dil, _, length, _ = zr.shape
    lc = min(length, 1024 if dil > 1 else Q_BLOCK * UNITS_PER_STEP)
    nres = min(dil, UNITS_PER_STEP // (lc // Q_BLOCK))
    return _attn_call((zr, 0), (zr, 1), (zr, 2), bias, None, halo=window // (2 * dil), lc=lc,
                      nres=nres, want_lse=True, name=name)


def _rms(x, g):
    return x * lax.rsqrt(jnp.mean(x * x, axis=-1, keepdims=True) + RMS_EPS) * g


def _tokens(src_ref, stage_ref, tm):
    dil = src_ref.shape[0]
    if dil == 1:
        return src_ref[0].astype(_F32)
    for r in range(dil):
        for k in range(QUAD // LANES):
            stage_ref[k, pl.ds(r, tm // dil, stride=dil), :] = (
                src_ref[r, :, k * LANES:(k + 1) * LANES].astype(_F32))
    return jnp.concatenate([stage_ref[k] for k in range(QUAD // LANES)], axis=-1)


def _tail_kernel(x_ref, gate_ref, oa_ref, ob1_ref, ob2_ref, ob3_ref, l1_ref, l2_ref, l3_ref,
                 wa_ref, wb_ref, wo_ref, gm_ref, wu_ref, wd_ref, gf_ref, y_ref,
                 so2_ref, so3_ref, sl2_ref, sl3_ref, *, tm):
    outs = [_tokens(ob1_ref, None, tm), _tokens(ob2_ref, so2_ref, tm), _tokens(ob3_ref, so3_ref, tm)]
    lses = [_tokens(l1_ref, None, tm), _tokens(l2_ref, sl2_ref, tm), _tokens(l3_ref, sl3_ref, tm)]
    top = jnp.maximum(jnp.maximum(lses[0], lses[1]), lses[2])
    ws = [jnp.exp(l - top) for l in lses]
    mix = ws[0] * outs[0] + ws[1] * outs[1] + ws[2] * outs[2]
    o_b = (mix / (ws[0] + ws[1] + ws[2])).astype(_BF16)

    def mixer(rows):
        pa = jnp.dot(oa_ref[rows, :], wa_ref[...], preferred_element_type=_F32)
        pb = jnp.dot(o_b[rows], wb_ref[...], preferred_element_type=_F32)
        merged = (gate_ref[rows, :D_MODEL].astype(_F32) * pa
                  + gate_ref[rows, D_MODEL:].astype(_F32) * pb).astype(_BF16)
        x1 = x_ref[rows, :] + jnp.dot(merged, wo_ref[...], preferred_element_type=_F32)
        return x1, _rms(x1, gm_ref[...]).astype(_BF16)

    def mlp(rows, x1, h):
        acc = x1
        for c in range(D_FF // D_MODEL):
            cols = slice(c * D_MODEL, (c + 1) * D_MODEL)
            u = jnp.maximum(jnp.dot(h, wu_ref[:, cols], preferred_element_type=_F32), 0.0)
            acc = acc + jnp.dot((u * u).astype(_BF16), wd_ref[cols, :], preferred_element_type=_F32)
        y_ref[rows, :] = _rms(acc, gf_ref[...])

    halves = [slice(i * (tm // 2), (i + 1) * (tm // 2)) for i in range(2)]
    mixed = [mixer(rows) for rows in halves]
    for rows, (x1, h) in zip(halves, mixed):
        mlp(rows, x1, h)


def _tail(x, gates, o_a, o_bs, lses, wa, wb, wo, g_mlp, wu, wd, g_final, tm):
    b, s, _ = x.shape
    const = lambda bi, i: (0, 0)
    row = lambda bi, i: (bi, i, 0)
    resident = lambda a: pl.BlockSpec(a.shape, const, pipeline_mode=pl.Buffered(1))
    res_specs = [pl.BlockSpec((None, dil, tm // dil, QUAD), lambda bi, i: (bi, 0, i, 0))
                 for _, dil in B_GROUPS]
    in_specs = (
        [pl.BlockSpec((None, tm, D_MODEL), row), pl.BlockSpec((None, tm, 2 * D_MODEL), row),
         pl.BlockSpec((None, tm, A_Q), row)]
        + res_specs + res_specs
        + [resident(wa), resident(wb), resident(wo), resident(g_mlp), resident(wu), resident(wd),
           resident(g_final)]
    )
    stage = pltpu.VMEM((QUAD // LANES, tm, LANES), _F32)
    return pl.pallas_call(
        functools.partial(_tail_kernel, tm=tm),
        grid=(b, s // tm),
        in_specs=in_specs,
        out_specs=pl.BlockSpec((None, tm, D_MODEL), row),
        out_shape=jax.ShapeDtypeStruct((b, s, D_MODEL), _F32),
        scratch_shapes=[stage] * 4,
        compiler_params=pltpu.CompilerParams(
            dimension_semantics=("arbitrary", "arbitrary"), vmem_limit_bytes=VMEM_LIMIT_BYTES),
        name="tail",
    )(x, gates, o_a, *o_bs, *lses, wa, wb, wo, g_mlp, wu, wd, g_final)


def _prepare(rel_bias, g_mix, w_in, b_gate, w_branch_a, w_branch_b, w_out, attn_sink, g_mlp,
             w_up, w_down, g_final):
    scale = HEAD_DIM ** -0.5 * LOG2E
    order = np.asarray(A_HEAD_ORDER)
    o1, o2, o3 = A_Q, A_Q + A_KV, A_Q + 2 * A_KV
    o4, o5 = o3 + B_W, o3 + 2 * B_W
    o6 = o3 + 3 * B_W
    qa = w_in[:, :o1].reshape(D_MODEL, A_HEADS, HEAD_DIM)[:, order].reshape(D_MODEL, A_Q) * scale
    slabs = [qa, w_in[:, o1:o2], w_in[:, o2:o3]]
    for gi in range(len(B_GROUPS)):
        cols = slice(gi * B_OUT, (gi + 1) * B_OUT)
        slabs += [w_in[:, o3:o4][:, cols] * scale, w_in[:, o4:o5][:, cols], w_in[:, o5:o6][:, cols]]
    w_cat = jnp.concatenate(slabs + [w_in[:, o6:]], axis=1).astype(_BF16)

    wa = w_branch_a.reshape(A_HEADS, HEAD_DIM, D_MODEL)[order].reshape(A_Q, D_MODEL).astype(_BF16)
    bias_a = _bias_blocks(rel_bias[:, :A_HEADS][:, order], A_WINDOW, 1, Q_BLOCK + 2 * A_WINDOW)
    bias_b = []
    for gi, (window, dil) in enumerate(B_GROUPS):
        n = window // (2 * dil)
        table = rel_bias[:, A_HEADS + gi * B_HEADS_PER_GROUP:A_HEADS + (gi + 1) * B_HEADS_PER_GROUP]
        bias_b.append(_bias_blocks(table, n, dil, Q_BLOCK + 2 * n))
    return dict(
        w_cat=w_cat, g_mix=g_mix.reshape(1, D_MODEL), b_gate=b_gate.reshape(1, 2 * D_MODEL),
        wa=wa, wb=w_branch_b.astype(_BF16), wo=w_out.astype(_BF16),
        sink=attn_sink[order].astype(_F32) * LOG2E, g_mlp=g_mlp.reshape(1, D_MODEL),
        wu=w_up.astype(_BF16), wd=w_down.astype(_BF16), g_final=g_final.reshape(1, D_MODEL),
        bias_a=bias_a, bias_b=bias_b,
    )


def _encode(x, p):
    qa, ka, va, zb1, zb2, zb3, gates = _inproj(x, p["g_mix"], p["w_cat"], p["b_gate"], tm=1024)
    o_a = _attn_a(qa, ka, va, p["bias_a"], p["sink"])
    o_bs, lses = [], []
    for gi, ((window, _), zr) in enumerate(zip(B_GROUPS, (zb1, zb2, zb3))):
        o, lse = _attn_b(zr, p["bias_b"][gi], window, f"attn_b{gi}")
        o_bs.append(o)
        lses.append(lse)
    return _tail(x, gates, o_a, o_bs, lses, p["wa"], p["wb"], p["wo"], p["g_mlp"], p["wu"], p["wd"],
                 p["g_final"], tm=512)


def kernel(x_prompt, x_sample, rel_bias, g_mix, w_in, b_gate, w_branch_a, w_branch_b, w_out,
           attn_sink, g_mlp, w_up, w_down, g_final):
    assert g_mix.shape[0] == 1, "single-layer encoder"
    p = _prepare(rel_bias, g_mix[0], w_in[0], b_gate[0], w_branch_a[0], w_branch_b[0], w_out[0],
                 attn_sink[0], g_mlp[0], w_up[0], w_down[0], g_final)
    return (_encode(x_prompt, p), _encode(x_sample, p))
```

```python
import functools
import math

import numpy as np
import jax
import jax.numpy as jnp
from jax import lax
from jax.experimental import pallas as pl
from jax.experimental.pallas import tpu as pltpu

D_MODEL = 1024
HEAD_DIM = 64
A_HEADS = 8
A_KV_HEADS = 2
A_WINDOW = 128
B_GROUPS = ((128, 1), (512, 4), (2048, 16))
B_HEADS_PER_GROUP = 4
B_HEADS = B_HEADS_PER_GROUP * len(B_GROUPS)
REL_BUCKETS = 32
REL_MAX_DISTANCE = 1024
D_FF = 4 * D_MODEL
A_Q = A_HEADS * HEAD_DIM
A_KV = A_KV_HEADS * HEAD_DIM
B_W = B_HEADS * HEAD_DIM
B_OUT = B_HEADS_PER_GROUP * HEAD_DIM
RMS_EPS = 1e-6
NEG_INF = -1e30
LOG2E = math.log2(math.e)
LN2 = math.log(2.0)

LANES = 128
QUAD = 4 * HEAD_DIM
PAIR = 2 * HEAD_DIM
SLAB = 3 * QUAD
Q_BLOCK = 128
FOLD = Q_BLOCK // 2
INPROJ_ROWS = 1024
SUB_ROWS = 256
TAIL_ROWS = 512
UNITS_PER_STEP = 32
DILATED_CHUNK_ROWS = 1024
VMEM_LIMIT_BYTES = 58 * 1024 * 1024

A_HEAD_ORDER = (0, 4, 1, 5, 2, 6, 3, 7)

_BF16 = jnp.bfloat16
_F32 = jnp.float32


def _rel_bucket(rel):
    half = REL_BUCKETS // 2
    max_exact = half // 2
    n = np.abs(rel)
    large = max_exact + (np.log(np.maximum(n, 1) / max_exact)
                         / np.log(REL_MAX_DISTANCE / max_exact) * (half - max_exact)).astype(np.int32)
    large = np.minimum(large, half - 1)
    return (np.where(rel > 0, half, 0) + np.where(n < max_exact, n, large)).astype(np.int32)


def _bias_blocks(table, n, dist_scale, nk):
    rel = np.arange(nk)[:, None] - n - np.arange(Q_BLOCK)[None, :]
    bucket = _rel_bucket(rel * dist_scale)
    table = table.astype(_F32) * LOG2E
    bias = jnp.zeros((table.shape[1], nk, Q_BLOCK), _F32)
    for bkt in np.unique(bucket):
        bias = jnp.where(jnp.asarray(bucket == bkt)[None], table[bkt][:, None, None], bias)
    band = np.abs(rel) <= n
    key = np.arange(nk)[:, None]
    variants = []
    for var in range(4):
        ok = band.copy()
        if var & 1:
            ok &= key >= n
        if var & 2:
            ok &= key < nk - n
        variants.append(jnp.where(ok[None], bias, NEG_INF))
    tiles = jnp.stack(variants)
    return jnp.concatenate(
        [jnp.where(np.arange(Q_BLOCK) < FOLD, tiles[..., :FOLD, :], tiles[..., nk - FOLD:, :]),
         tiles[..., FOLD:nk - FOLD, :]], axis=-2)


def _inproj_kernel(x_ref, g_ref, w_ref, b_ref, qa_ref, ka_ref, va_ref, zb1_ref, zb2_ref, zb3_ref,
                   gate_ref, zs_ref, *, tm):
    x = x_ref[...]
    ms = jnp.mean(x * x, axis=-1, keepdims=True)
    h = (x * lax.rsqrt(ms + RMS_EPS) * g_ref[...]).astype(_BF16)
    for sub in range(tm // SUB_ROWS):
        rows = slice(sub * SUB_ROWS, (sub + 1) * SUB_ROWS)
        hs = h[rows]
        zg = jnp.dot(hs, w_ref[:, 4 * SLAB:], preferred_element_type=_F32) + b_ref[...]
        gate_ref[rows, :] = (0.5 * jnp.tanh(0.5 * zg) + 0.5).astype(_BF16)
        for gi in reversed(range(len(B_GROUPS))):
            out, dil = (zb1_ref, zb2_ref, zb3_ref)[gi], B_GROUPS[gi][1]
            z = jnp.dot(hs, w_ref[:, (gi + 1) * SLAB:(gi + 2) * SLAB], preferred_element_type=_F32)
            if dil == 1:
                for part in range(3):
                    out[0, part, rows, :] = z[:, part * QUAD:(part + 1) * QUAD].astype(_BF16)
                continue
            for k in range(SLAB // LANES):
                zs_ref[k, rows, :] = z[:, k * LANES:(k + 1) * LANES]
            per = SUB_ROWS // dil
            for r in range(dil):
                for k in range(SLAB // LANES):
                    part, half = divmod(k, QUAD // LANES)
                    out[r, part, sub * per:(sub + 1) * per, half * LANES:(half + 1) * LANES] = (
                        zs_ref[k, pl.ds(sub * SUB_ROWS + r, per, stride=dil), :].astype(_BF16))
        za = jnp.dot(hs, w_ref[:, :SLAB], preferred_element_type=_F32).astype(_BF16)
        qa_ref[rows, :] = za[:, :A_Q]
        ka_ref[rows, :] = za[:, A_Q:A_Q + A_KV]
        va_ref[rows, :] = za[:, A_Q + A_KV:]


def _inproj(x, g_mix, w_cat, b_gate, tm):
    b, s, _ = x.shape
    const = lambda bi, i: (0, 0)
    row = lambda bi, i: (bi, i, 0)
    res_specs = [pl.BlockSpec((None, dil, 3, tm // dil, QUAD), lambda bi, i: (bi, 0, 0, i, 0))
                 for _, dil in B_GROUPS]
    res_shapes = [jax.ShapeDtypeStruct((b, dil, 3, s // dil, QUAD), _BF16) for _, dil in B_GROUPS]
    a_widths = (A_Q, A_KV, A_KV)
    return pl.pallas_call(
        functools.partial(_inproj_kernel, tm=tm),
        grid=(b, s // tm),
        in_specs=[
            pl.BlockSpec((None, tm, D_MODEL), row),
            pl.BlockSpec((1, D_MODEL), const),
            pl.BlockSpec(w_cat.shape, const, pipeline_mode=pl.Buffered(1)),
            pl.BlockSpec((1, 2 * D_MODEL), const),
        ],
        out_specs=[pl.BlockSpec((None, tm, w), row) for w in a_widths] + res_specs
        + [pl.BlockSpec((None, tm, 2 * D_MODEL), row)],
        out_shape=[jax.ShapeDtypeStruct((b, s, w), _BF16) for w in a_widths] + res_shapes
        + [jax.ShapeDtypeStruct((b, s, 2 * D_MODEL), _BF16)],
        scratch_shapes=[pltpu.VMEM((SLAB // LANES, tm, LANES), _F32)],
        compiler_params=pltpu.CompilerParams(
            dimension_semantics=("arbitrary", "arbitrary"), vmem_limit_bytes=VMEM_LIMIT_BYTES),
        name="inproj",
    )(x, g_mix, w_cat, b_gate)


def _attn_ops(q_ref, kv_refs, bias_ref, sink_ref, stage_refs, write, *, halo, lc, kv_pairs,
              first, last, want_lse):
    kc_ref, vc_ref, kp_ref, vp_ref, kn_ref, vn_ref = kv_refs
    kext_ref, vext_ref, vt_ref = stage_refs
    nk = Q_BLOCK + 2 * halo
    nblk = lc // Q_BLOCK
    qgrp = lax.broadcasted_iota(jnp.int32, (Q_BLOCK, QUAD), 1) // HEAD_DIM
    low_queries = lax.broadcasted_iota(jnp.int32, (FOLD, Q_BLOCK), 1) < FOLD

    def stage(r):
        pieces = ((0, halo, kp_ref, vp_ref), (halo, lc, kc_ref, vc_ref),
                  (halo + lc, halo, kn_ref, vn_ref))
        for lo, rows, k_src, v_src in pieces:
            k = k_src[r]
            kext_ref[r, lo:lo + rows, :] = k if kv_pairs == 2 else jnp.concatenate([k, k], axis=-1)
            vext_ref[r, lo:lo + rows, :] = v_src[r]
        vt_ref[r] = vext_ref[r].astype(_F32).T.astype(_BF16)

    def scores(r, i, quad):
        if i == 0 and quad == 0:
            stage(r)
        r0 = i * Q_BLOCK
        q4 = q_ref[r, r0:r0 + Q_BLOCK, quad * QUAD:(quad + 1) * QUAD]
        qs = jnp.concatenate(
            [jnp.where(qgrp == j, q4, jnp.zeros_like(q4)) for j in range(4)], axis=0)
        return lax.dot_general(kext_ref[r, r0:r0 + nk, :], qs, (((1,), (1,)), ((), ())),
                               preferred_element_type=_F32)

    def finish(r, i, quad, st):
        r0 = i * Q_BLOCK
        var = (first if i == 0 else 0) + (last if i == nblk - 1 else 0)
        pts, scales, lses = [], [], []
        for j in range(4):
            head = quad * 4 + j
            s = st[:, j * Q_BLOCK:(j + 1) * Q_BLOCK]
            s = jnp.concatenate(
                [jnp.where(low_queries, s[:FOLD], s[nk - FOLD:]), s[FOLD:nk - FOLD]], axis=0)
            logits = s + bias_ref[var, head]
            m = jnp.max(logits, axis=0, keepdims=True)
            if sink_ref is not None:
                sink = sink_ref[head]
                m = jnp.maximum(m, sink)
            p = jnp.exp2(logits - m)
            den = jnp.sum(p, axis=0, keepdims=True)
            if sink_ref is not None:
                den = den + jnp.exp2(sink - m)
            shared, zero = p[:FOLD], jnp.zeros((FOLD, Q_BLOCK), _F32)
            p = jnp.concatenate([jnp.where(low_queries, shared, zero), p[FOLD:],
                                 jnp.where(low_queries, zero, shared)], axis=0)
            pts.append(p.astype(_BF16))
            scales.append(1.0 / den)
            lses.append(m * LN2 + jnp.log(den))
        outs = []
        for pair in range(2):
            j0 = 2 * pair
            vrows = (pair % kv_pairs) * PAIR
            ot = jnp.dot(vt_ref[r, vrows:vrows + PAIR, r0:r0 + nk],
                         jnp.concatenate(pts[j0:j0 + 2], axis=1), preferred_element_type=_F32)
            outs.append(ot[:HEAD_DIM, :Q_BLOCK] * scales[j0])
            outs.append(ot[HEAD_DIM:, Q_BLOCK:] * scales[j0 + 1])
        lse = None
        if want_lse:
            lse = jnp.concatenate(
                [jnp.broadcast_to(l, (HEAD_DIM, Q_BLOCK)) for l in lses], axis=0).T
        write(r, r0, quad, jnp.concatenate(outs, axis=0).T.astype(_BF16), lse)

    return scores, finish


def _attn_kernel(*refs, halo, lc, nres, nquad, kv_pairs, has_sink, want_lse, chunk_axis):
    it = iter(refs)
    q_ref = next(it)
    kv_refs = [next(it) for _ in range(6)]
    bias_ref = next(it)
    sink_ref = next(it) if has_sink else None
    o_ref = next(it)
    lse_ref = next(it) if want_lse else None
    stage_refs = [next(it) for _ in range(3)]

    c = pl.program_id(chunk_axis)
    first = jnp.where(c == 0, 1, 0)
    last = jnp.where(c == pl.num_programs(chunk_axis) - 1, 2, 0)

    def write(r, r0, quad, o, lse):
        o_ref[r, r0:r0 + Q_BLOCK, quad * QUAD:(quad + 1) * QUAD] = o
        if want_lse:
            lse_ref[r, r0:r0 + Q_BLOCK, :] = lse

    scores, finish = _attn_ops(q_ref, kv_refs, bias_ref, sink_ref, stage_refs, write, halo=halo,
                               lc=lc, kv_pairs=kv_pairs, first=first, last=last, want_lse=want_lse)

    units = [(r, i, quad) for r in range(nres) for i in range(lc // Q_BLOCK) for quad in range(nquad)]
    ahead = 2 if nquad == 1 else 1
    pending = [scores(*u) for u in units[:ahead]]
    for idx, unit in enumerate(units):
        if idx + ahead < len(units):
            pending.append(scores(*units[idx + ahead]))
        finish(*unit, pending.pop(0))


def _attn_call(q, k, v, bias, sink, *, halo, lc, nres, want_lse, name):
    def operand(x):
        arr, part = x if isinstance(x, tuple) else (x, None)
        width = arr.shape[-1]

        def spec(rows, row_block):
            if part is None:
                return pl.BlockSpec((None, nres, rows, width),
                                    lambda bi, r, c: (bi, r, row_block(c), 0))
            return pl.BlockSpec((None, nres, None, rows, width),
                                lambda bi, r, c: (bi, r, part, row_block(c), 0))
        return arr, spec

    (q_arr, q_spec), (k_arr, k_spec), (v_arr, v_spec) = operand(q), operand(k), operand(v)
    b, dil, length, qw, kvw = q_arr.shape[0], q_arr.shape[1], q_arr.shape[-2], q_arr.shape[-1], k_arr.shape[-1]
    per_chunk = lc // halo
    ext = lc + 2 * halo
    here = lambda c: c
    prev = lambda c: jnp.maximum(c * per_chunk - 1, 0)
    nxt = lambda c: jnp.minimum((c + 1) * per_chunk, length // halo - 1)
    in_specs = [
        q_spec(lc, here), k_spec(lc, here), v_spec(lc, here),
        k_spec(halo, prev), v_spec(halo, prev), k_spec(halo, nxt), v_spec(halo, nxt),
        pl.BlockSpec(bias.shape, lambda bi, r, c: (0, 0, 0, 0), pipeline_mode=pl.Buffered(1)),
    ]
    args = [q_arr, k_arr, v_arr, k_arr, v_arr, k_arr, v_arr, bias]
    if sink is not None:
        in_specs.append(pl.BlockSpec(memory_space=pltpu.SMEM))
        args.append(sink)
    out_spec = pl.BlockSpec((None, nres, lc, qw), lambda bi, r, c: (bi, r, c, 0))
    out_specs, out_shape = [out_spec], [jax.ShapeDtypeStruct((b, dil, length, qw), _BF16)]
    if want_lse:
        out_specs.append(out_spec)
        out_shape.append(jax.ShapeDtypeStruct((b, dil, length, qw), _F32))
    kernel = functools.partial(
        _attn_kernel, halo=halo, lc=lc, nres=nres, nquad=qw // QUAD, kv_pairs=kvw // PAIR,
        has_sink=sink is not None, want_lse=want_lse, chunk_axis=2)
    grid = (b, dil // nres, length // lc)
    return pl.pallas_call(
        kernel,
        grid=grid,
        in_specs=in_specs,
        out_specs=out_specs,
        out_shape=out_shape,
        scratch_shapes=[
            pltpu.VMEM((nres, ext, QUAD), _BF16),
            pltpu.VMEM((nres, ext, kvw), _BF16),
            pltpu.VMEM((nres, kvw, ext), _BF16),
        ],
        compiler_params=pltpu.CompilerParams(
            dimension_semantics=("arbitrary",) * len(grid), vmem_limit_bytes=VMEM_LIMIT_BYTES),
        name=name,
    )(*args)


def _attn_a(qa, ka, va, bias, sink):
    b, s, _ = qa.shape
    lc = min(s, Q_BLOCK * UNITS_PER_STEP // (A_Q // QUAD))
    seq = lambda a: a.reshape(b, 1, s, a.shape[-1])
    (o,) = _attn_call(seq(qa), seq(ka), seq(va), bias, sink, halo=A_WINDOW, lc=lc, nres=1,
                      want_lse=False, name="attn_a")
    return o.reshape(b, s, A_Q)


def _attn_b(zr, bias, window, name):
    _, dil, _, length, _ = zr.shape
    lc = min(length, DILATED_CHUNK_ROWS if dil > 1 else Q_BLOCK * UNITS_PER_STEP)
    nres = min(dil, UNITS_PER_STEP // (lc // Q_BLOCK))
    return _attn_call((zr, 0), (zr, 1), (zr, 2), bias, None, halo=window // (2 * dil), lc=lc,
                      nres=nres, want_lse=True, name=name)


def _rms(x, g):
    return x * lax.rsqrt(jnp.mean(x * x, axis=-1, keepdims=True) + RMS_EPS) * g


def _tokens(src_ref, stage_ref, tm):
    dil = src_ref.shape[0]
    if dil == 1:
        return src_ref[0].astype(_F32)
    for r in range(dil):
        for k in range(QUAD // LANES):
            stage_ref[k, pl.ds(r, tm // dil, stride=dil), :] = (
                src_ref[r, :, k * LANES:(k + 1) * LANES].astype(_F32))
    return jnp.concatenate([stage_ref[k] for k in range(QUAD // LANES)], axis=-1)


def _tail_kernel(x_ref, gate_ref, oa_ref, ob1_ref, ob2_ref, ob3_ref, l1_ref, l2_ref, l3_ref,
                 wa_ref, wb_ref, wo_ref, gm_ref, wu_ref, wd_ref, gf_ref, y_ref,
                 so2_ref, so3_ref, sl2_ref, sl3_ref, *, tm):
    outs = [_tokens(ob1_ref, None, tm), _tokens(ob2_ref, so2_ref, tm), _tokens(ob3_ref, so3_ref, tm)]
    lses = [_tokens(l1_ref, None, tm), _tokens(l2_ref, sl2_ref, tm), _tokens(l3_ref, sl3_ref, tm)]
    top = jnp.maximum(jnp.maximum(lses[0], lses[1]), lses[2])
    ws = [jnp.exp(l - top) for l in lses]
    mix = ws[0] * outs[0] + ws[1] * outs[1] + ws[2] * outs[2]
    o_b = (mix / (ws[0] + ws[1] + ws[2])).astype(_BF16)

    def mixer(rows):
        pa = jnp.dot(oa_ref[rows, :], wa_ref[...], preferred_element_type=_F32)
        pb = jnp.dot(o_b[rows], wb_ref[...], preferred_element_type=_F32)
        merged = (gate_ref[rows, :D_MODEL].astype(_F32) * pa
                  + gate_ref[rows, D_MODEL:].astype(_F32) * pb).astype(_BF16)
        x1 = x_ref[rows, :] + jnp.dot(merged, wo_ref[...], preferred_element_type=_F32)
        return x1, _rms(x1, gm_ref[...]).astype(_BF16)

    def mlp(rows, x1, h):
        acc = x1
        for c in range(D_FF // D_MODEL):
            cols = slice(c * D_MODEL, (c + 1) * D_MODEL)
            u = jnp.maximum(jnp.dot(h, wu_ref[:, cols], preferred_element_type=_F32), 0.0)
            acc = acc + jnp.dot((u * u).astype(_BF16), wd_ref[cols, :], preferred_element_type=_F32)
        y_ref[rows, :] = _rms(acc, gf_ref[...])

    halves = [slice(i * (tm // 2), (i + 1) * (tm // 2)) for i in range(2)]
    mixed = [mixer(rows) for rows in halves]
    for rows, (x1, h) in zip(halves, mixed):
        mlp(rows, x1, h)


def _tail(x, gates, o_a, o_bs, lses, wa, wb, wo, g_mlp, wu, wd, g_final, tm):
    b, s, _ = x.shape
    const = lambda bi, i: (0, 0)
    row = lambda bi, i: (bi, i, 0)
    resident = lambda a: pl.BlockSpec(a.shape, const, pipeline_mode=pl.Buffered(1))
    res_specs = [pl.BlockSpec((None, dil, tm // dil, QUAD), lambda bi, i: (bi, 0, i, 0))
                 for _, dil in B_GROUPS]
    in_specs = (
        [pl.BlockSpec((None, tm, D_MODEL), row), pl.BlockSpec((None, tm, 2 * D_MODEL), row),
         pl.BlockSpec((None, tm, A_Q), row)]
        + res_specs + res_specs
        + [resident(wa), resident(wb), resident(wo), resident(g_mlp), resident(wu), resident(wd),
           resident(g_final)]
    )
    stage = pltpu.VMEM((QUAD // LANES, tm, LANES), _F32)
    return pl.pallas_call(
        functools.partial(_tail_kernel, tm=tm),
        grid=(b, s // tm),
        in_specs=in_specs,
        out_specs=pl.BlockSpec((None, tm, D_MODEL), row),
        out_shape=jax.ShapeDtypeStruct((b, s, D_MODEL), _F32),
        scratch_shapes=[stage] * 4,
        compiler_params=pltpu.CompilerParams(
            dimension_semantics=("arbitrary", "arbitrary"), vmem_limit_bytes=VMEM_LIMIT_BYTES),
        name="tail",
    )(x, gates, o_a, *o_bs, *lses, wa, wb, wo, g_mlp, wu, wd, g_final)


def _prepare(rel_bias, g_mix, w_in, b_gate, w_branch_a, w_branch_b, w_out, attn_sink, g_mlp,
             w_up, w_down, g_final):
    scale = HEAD_DIM ** -0.5 * LOG2E
    order = np.asarray(A_HEAD_ORDER)
    o1, o2, o3 = A_Q, A_Q + A_KV, A_Q + 2 * A_KV
    o4, o5 = o3 + B_W, o3 + 2 * B_W
    o6 = o3 + 3 * B_W
    qa = w_in[:, :o1].reshape(D_MODEL, A_HEADS, HEAD_DIM)[:, order].reshape(D_MODEL, A_Q) * scale
    slabs = [qa, w_in[:, o1:o2], w_in[:, o2:o3]]
    for gi in range(len(B_GROUPS)):
        cols = slice(gi * B_OUT, (gi + 1) * B_OUT)
        slabs += [w_in[:, o3:o4][:, cols] * scale, w_in[:, o4:o5][:, cols], w_in[:, o5:o6][:, cols]]
    w_cat = jnp.concatenate(slabs + [w_in[:, o6:]], axis=1).astype(_BF16)

    wa = w_branch_a.reshape(A_HEADS, HEAD_DIM, D_MODEL)[order].reshape(A_Q, D_MODEL).astype(_BF16)
    bias_a = _bias_blocks(rel_bias[:, :A_HEADS][:, order], A_WINDOW, 1, Q_BLOCK + 2 * A_WINDOW)
    bias_b = []
    for gi, (window, dil) in enumerate(B_GROUPS):
        n = window // (2 * dil)
        table = rel_bias[:, A_HEADS + gi * B_HEADS_PER_GROUP:A_HEADS + (gi + 1) * B_HEADS_PER_GROUP]
        bias_b.append(_bias_blocks(table, n, dil, Q_BLOCK + 2 * n))
    return dict(
        w_cat=w_cat, g_mix=g_mix.reshape(1, D_MODEL), b_gate=b_gate.reshape(1, 2 * D_MODEL),
        wa=wa, wb=w_branch_b.astype(_BF16), wo=w_out.astype(_BF16),
        sink=attn_sink[order].astype(_F32) * LOG2E, g_mlp=g_mlp.reshape(1, D_MODEL),
        wu=w_up.astype(_BF16), wd=w_down.astype(_BF16), g_final=g_final.reshape(1, D_MODEL),
        bias_a=bias_a, bias_b=bias_b,
    )


def _encode(x, p):
    qa, ka, va, zb1, zb2, zb3, gates = _inproj(x, p["g_mix"], p["w_cat"], p["b_gate"], tm=INPROJ_ROWS)
    o_a = _attn_a(qa, ka, va, p["bias_a"], p["sink"])
    o_bs, lses = [], []
    for gi, ((window, _), zr) in enumerate(zip(B_GROUPS, (zb1, zb2, zb3))):
        o, lse = _attn_b(zr, p["bias_b"][gi], window, f"attn_b{gi}")
        o_bs.append(o)
        lses.append(lse)
    return _tail(x, gates, o_a, o_bs, lses, p["wa"], p["wb"], p["wo"], p["g_mlp"], p["wu"], p["wd"],
                 p["g_final"], tm=TAIL_ROWS)


def kernel(x_prompt, x_sample, rel_bias, g_mix, w_in, b_gate, w_branch_a, w_branch_b, w_out,
           attn_sink, g_mlp, w_up, w_down, g_final):
    assert g_mix.shape[0] == 1, "single-layer encoder"
    p = _prepare(rel_bias, g_mix[0], w_in[0], b_gate[0], w_branch_a[0], w_branch_b[0], w_out[0],
                 attn_sink[0], g_mlp[0], w_up[0], w_down[0], g_final)
    return (_encode(x_prompt, p), _encode(x_sample, p))
```

```python
import functools
import math

import numpy as np
import jax
import jax.numpy as jnp
from jax import lax
from jax.experimental import pallas as pl
from jax.experimental.pallas import tpu as pltpu

D_MODEL = 1024
HEAD_DIM = 64
A_HEADS = 8
A_KV_HEADS = 2
A_WINDOW = 128
B_GROUPS = ((128, 1), (512, 4), (2048, 16))
B_HEADS_PER_GROUP = 4
B_HEADS = B_HEADS_PER_GROUP * len(B_GROUPS)
REL_BUCKETS = 32
REL_MAX_DISTANCE = 1024
D_FF = 4 * D_MODEL
A_Q = A_HEADS * HEAD_DIM
A_KV = A_KV_HEADS * HEAD_DIM
B_W = B_HEADS * HEAD_DIM
B_OUT = B_HEADS_PER_GROUP * HEAD_DIM
RMS_EPS = 1e-6
NEG_INF = -1e30
LOG2E = math.log2(math.e)
LN2 = math.log(2.0)

LANES = 128
QUAD = 4 * HEAD_DIM
PAIR = 2 * HEAD_DIM
SLAB = 3 * QUAD
Q_BLOCK = 128
FOLD = Q_BLOCK // 2
INPROJ_ROWS = 1024
SUB_ROWS = 256
TAIL_ROWS = 512
UNITS_PER_STEP = 32
DILATED_CHUNK_ROWS = 1024
VMEM_LIMIT_BYTES = 58 * 1024 * 1024

A_HEAD_ORDER = (0, 4, 1, 5, 2, 6, 3, 7)

_BF16 = jnp.bfloat16
_F32 = jnp.float32


def _rel_bucket(rel):
    half = REL_BUCKETS // 2
    max_exact = half // 2
    n = np.abs(rel)
    large = max_exact + (np.log(np.maximum(n, 1) / max_exact)
                         / np.log(REL_MAX_DISTANCE / max_exact) * (half - max_exact)).astype(np.int32)
    large = np.minimum(large, half - 1)
    return (np.where(rel > 0, half, 0) + np.where(n < max_exact, n, large)).astype(np.int32)


def _bias_blocks(table, n, dist_scale, nk):
    rel = np.arange(nk)[:, None] - n - np.arange(Q_BLOCK)[None, :]
    bucket = _rel_bucket(rel * dist_scale)
    table = table.astype(_F32) * LOG2E
    bias = jnp.zeros((table.shape[1], nk, Q_BLOCK), _F32)
    for bkt in np.unique(bucket):
        bias = jnp.where(jnp.asarray(bucket == bkt)[None], table[bkt][:, None, None], bias)
    band = np.abs(rel) <= n
    key = np.arange(nk)[:, None]
    variants = []
    for var in range(4):
        ok = band.copy()
        if var & 1:
            ok &= key >= n
        if var & 2:
            ok &= key < nk - n
        variants.append(jnp.where(ok[None], bias, NEG_INF))
    tiles = jnp.stack(variants)
    return jnp.concatenate(
        [jnp.where(np.arange(Q_BLOCK) < FOLD, tiles[..., :FOLD, :], tiles[..., nk - FOLD:, :]),
         tiles[..., FOLD:nk - FOLD, :]], axis=-2)


def _inproj_kernel(x_ref, g_ref, w_ref, qa_ref, ka_ref, va_ref, zb1_ref, zb2_ref, zb3_ref, zs_ref,
                   *, tm):
    x = x_ref[...]
    ms = jnp.mean(x * x, axis=-1, keepdims=True)
    h = (x * lax.rsqrt(ms + RMS_EPS) * g_ref[...]).astype(_BF16)
    for sub in range(tm // SUB_ROWS):
        rows = slice(sub * SUB_ROWS, (sub + 1) * SUB_ROWS)
        hs = h[rows]
        for gi in reversed(range(len(B_GROUPS))):
            out, dil = (zb1_ref, zb2_ref, zb3_ref)[gi], B_GROUPS[gi][1]
            z = jnp.dot(hs, w_ref[:, (gi + 1) * SLAB:(gi + 2) * SLAB], preferred_element_type=_F32)
            if dil == 1:
                for part in range(3):
                    out[0, part, rows, :] = z[:, part * QUAD:(part + 1) * QUAD].astype(_BF16)
                continue
            for k in range(SLAB // LANES):
                zs_ref[k, rows, :] = z[:, k * LANES:(k + 1) * LANES]
            per = SUB_ROWS // dil
            for r in range(dil):
                for k in range(SLAB // LANES):
                    part, half = divmod(k, QUAD // LANES)
                    out[r, part, sub * per:(sub + 1) * per, half * LANES:(half + 1) * LANES] = (
                        zs_ref[k, pl.ds(sub * SUB_ROWS + r, per, stride=dil), :].astype(_BF16))
        za = jnp.dot(hs, w_ref[:, :SLAB], preferred_element_type=_F32).astype(_BF16)
        qa_ref[rows, :] = za[:, :A_Q]
        ka_ref[rows, :] = za[:, A_Q:A_Q + A_KV]
        va_ref[rows, :] = za[:, A_Q + A_KV:]


def _inproj(x, g_mix, w_cat, tm):
    b, s, _ = x.shape
    const = lambda bi, i: (0, 0)
    row = lambda bi, i: (bi, i, 0)
    res_specs = [pl.BlockSpec((None, dil, 3, tm // dil, QUAD), lambda bi, i: (bi, 0, 0, i, 0))
                 for _, dil in B_GROUPS]
    res_shapes = [jax.ShapeDtypeStruct((b, dil, 3, s // dil, QUAD), _BF16) for _, dil in B_GROUPS]
    a_widths = (A_Q, A_KV, A_KV)
    return pl.pallas_call(
        functools.partial(_inproj_kernel, tm=tm),
        grid=(b, s // tm),
        in_specs=[
            pl.BlockSpec((None, tm, D_MODEL), row),
            pl.BlockSpec((1, D_MODEL), const),
            pl.BlockSpec(w_cat.shape, const, pipeline_mode=pl.Buffered(1)),
        ],
        out_specs=[pl.BlockSpec((None, tm, w), row) for w in a_widths] + res_specs,
        out_shape=[jax.ShapeDtypeStruct((b, s, w), _BF16) for w in a_widths] + res_shapes,
        scratch_shapes=[pltpu.VMEM((SLAB // LANES, tm, LANES), _F32)],
        compiler_params=pltpu.CompilerParams(
            dimension_semantics=("arbitrary", "arbitrary"), vmem_limit_bytes=VMEM_LIMIT_BYTES),
        name="inproj",
    )(x, g_mix, w_cat)


def _attn_ops(q_ref, kv_refs, bias_ref, sink_ref, stage_refs, write, *, halo, lc, kv_pairs,
              first, last, want_lse):
    kc_ref, vc_ref, kp_ref, vp_ref, kn_ref, vn_ref = kv_refs
    kext_ref, vext_ref, vt_ref = stage_refs
    nk = Q_BLOCK + 2 * halo
    nblk = lc // Q_BLOCK
    qgrp = lax.broadcasted_iota(jnp.int32, (Q_BLOCK, QUAD), 1) // HEAD_DIM
    low_queries = lax.broadcasted_iota(jnp.int32, (FOLD, Q_BLOCK), 1) < FOLD

    def stage(r):
        pieces = ((0, halo, kp_ref, vp_ref), (halo, lc, kc_ref, vc_ref),
                  (halo + lc, halo, kn_ref, vn_ref))
        for lo, rows, k_src, v_src in pieces:
            k = k_src[r]
            kext_ref[r, lo:lo + rows, :] = k if kv_pairs == 2 else jnp.concatenate([k, k], axis=-1)
            vext_ref[r, lo:lo + rows, :] = v_src[r]
        vt_ref[r] = vext_ref[r].astype(_F32).T.astype(_BF16)

    def scores(r, i, quad):
        if i == 0 and quad == 0:
            stage(r)
        r0 = i * Q_BLOCK
        q4 = q_ref[r, r0:r0 + Q_BLOCK, quad * QUAD:(quad + 1) * QUAD]
        qs = jnp.concatenate(
            [jnp.where(qgrp == j, q4, jnp.zeros_like(q4)) for j in range(4)], axis=0)
        return lax.dot_general(kext_ref[r, r0:r0 + nk, :], qs, (((1,), (1,)), ((), ())),
                               preferred_element_type=_F32)

    def finish(r, i, quad, st):
        r0 = i * Q_BLOCK
        var = (first if i == 0 else 0) + (last if i == nblk - 1 else 0)
        pts, scales, lses = [], [], []
        for j in range(4):
            head = quad * 4 + j
            s = st[:, j * Q_BLOCK:(j + 1) * Q_BLOCK]
            s = jnp.concatenate(
                [jnp.where(low_queries, s[:FOLD], s[nk - FOLD:]), s[FOLD:nk - FOLD]], axis=0)
            logits = s + bias_ref[var, head]
            m = jnp.max(logits, axis=0, keepdims=True)
            if sink_ref is not None:
                sink = sink_ref[head]
                m = jnp.maximum(m, sink)
            p = jnp.exp2(logits - m)
            den = jnp.sum(p, axis=0, keepdims=True)
            if sink_ref is not None:
                den = den + jnp.exp2(sink - m)
            shared, zero = p[:FOLD], jnp.zeros((FOLD, Q_BLOCK), _F32)
            p = jnp.concatenate([jnp.where(low_queries, shared, zero), p[FOLD:],
                                 jnp.where(low_queries, zero, shared)], axis=0)
            pts.append(p.astype(_BF16))
            scales.append(1.0 / den)
            lses.append(m * LN2 + jnp.log(den))
        outs = []
        for pair in range(2):
            j0 = 2 * pair
            vrows = (pair % kv_pairs) * PAIR
            ot = jnp.dot(vt_ref[r, vrows:vrows + PAIR, r0:r0 + nk],
                         jnp.concatenate(pts[j0:j0 + 2], axis=1), preferred_element_type=_F32)
            outs.append(ot[:HEAD_DIM, :Q_BLOCK] * scales[j0])
            outs.append(ot[HEAD_DIM:, Q_BLOCK:] * scales[j0 + 1])
        lse = None
        if want_lse:
            lse = jnp.concatenate(
                [jnp.broadcast_to(l, (HEAD_DIM, Q_BLOCK)) for l in lses], axis=0).T
        write(r, r0, quad, jnp.concatenate(outs, axis=0).T.astype(_BF16), lse)

    return scores, finish


def _attn_kernel(*refs, halo, lc, nres, nquad, kv_pairs, has_sink, want_lse, chunk_axis):
    it = iter(refs)
    q_ref = next(it)
    kv_refs = [next(it) for _ in range(6)]
    bias_ref = next(it)
    sink_ref = next(it) if has_sink else None
    o_ref = next(it)
    lse_ref = next(it) if want_lse else None
    stage_refs = [next(it) for _ in range(3)]

    c = pl.program_id(chunk_axis)
    first = jnp.where(c == 0, 1, 0)
    last = jnp.where(c == pl.num_programs(chunk_axis) - 1, 2, 0)

    def write(r, r0, quad, o, lse):
        o_ref[r, r0:r0 + Q_BLOCK, quad * QUAD:(quad + 1) * QUAD] = o
        if want_lse:
            lse_ref[r, r0:r0 + Q_BLOCK, :] = lse

    scores, finish = _attn_ops(q_ref, kv_refs, bias_ref, sink_ref, stage_refs, write, halo=halo,
                               lc=lc, kv_pairs=kv_pairs, first=first, last=last, want_lse=want_lse)

    units = [(r, i, quad) for r in range(nres) for i in range(lc // Q_BLOCK) for quad in range(nquad)]
    ahead = 2 if nquad == 1 else 1
    pending = [scores(*u) for u in units[:ahead]]
    for idx, unit in enumerate(units):
        if idx + ahead < len(units):
            pending.append(scores(*units[idx + ahead]))
        finish(*unit, pending.pop(0))


def _attn_call(q, k, v, bias, sink, *, halo, lc, nres, want_lse, name):
    def operand(x):
        arr, part = x if isinstance(x, tuple) else (x, None)
        width = arr.shape[-1]

        def spec(rows, row_block):
            if part is None:
                return pl.BlockSpec((None, nres, rows, width),
                                    lambda bi, r, c: (bi, r, row_block(c), 0))
            return pl.BlockSpec((None, nres, None, rows, width),
                                lambda bi, r, c: (bi, r, part, row_block(c), 0))
        return arr, spec

    (q_arr, q_spec), (k_arr, k_spec), (v_arr, v_spec) = operand(q), operand(k), operand(v)
    b, dil, length, qw, kvw = q_arr.shape[0], q_arr.shape[1], q_arr.shape[-2], q_arr.shape[-1], k_arr.shape[-1]
    per_chunk = lc // halo
    ext = lc + 2 * halo
    here = lambda c: c
    prev = lambda c: jnp.maximum(c * per_chunk - 1, 0)
    nxt = lambda c: jnp.minimum((c + 1) * per_chunk, length // halo - 1)
    in_specs = [
        q_spec(lc, here), k_spec(lc, here), v_spec(lc, here),
        k_spec(halo, prev), v_spec(halo, prev), k_spec(halo, nxt), v_spec(halo, nxt),
        pl.BlockSpec(bias.shape, lambda bi, r, c: (0, 0, 0, 0), pipeline_mode=pl.Buffered(1)),
    ]
    args = [q_arr, k_arr, v_arr, k_arr, v_arr, k_arr, v_arr, bias]
    if sink is not None:
        in_specs.append(pl.BlockSpec(memory_space=pltpu.SMEM))
        args.append(sink)
    out_spec = pl.BlockSpec((None, nres, lc, qw), lambda bi, r, c: (bi, r, c, 0))
    out_specs, out_shape = [out_spec], [jax.ShapeDtypeStruct((b, dil, length, qw), _BF16)]
    if want_lse:
        out_specs.append(out_spec)
        out_shape.append(jax.ShapeDtypeStruct((b, dil, length, qw), _F32))
    kernel = functools.partial(
        _attn_kernel, halo=halo, lc=lc, nres=nres, nquad=qw // QUAD, kv_pairs=kvw // PAIR,
        has_sink=sink is not None, want_lse=want_lse, chunk_axis=2)
    grid = (b, dil // nres, length // lc)
    return pl.pallas_call(
        kernel,
        grid=grid,
        in_specs=in_specs,
        out_specs=out_specs,
        out_shape=out_shape,
        scratch_shapes=[
            pltpu.VMEM((nres, ext, QUAD), _BF16),
            pltpu.VMEM((nres, ext, kvw), _BF16),
            pltpu.VMEM((nres, kvw, ext), _BF16),
        ],
        compiler_params=pltpu.CompilerParams(
            dimension_semantics=("arbitrary",) * len(grid), vmem_limit_bytes=VMEM_LIMIT_BYTES),
        name=name,
    )(*args)


def _attn_a(qa, ka, va, bias, sink):
    b, s, _ = qa.shape
    lc = min(s, Q_BLOCK * UNITS_PER_STEP // (A_Q // QUAD))
    seq = lambda a: a.reshape(b, 1, s, a.shape[-1])
    (o,) = _attn_call(seq(qa), seq(ka), seq(va), bias, sink, halo=A_WINDOW, lc=lc, nres=1,
                      want_lse=False, name="attn_a")
    return o.reshape(b, s, A_Q)


def _attn_b(zr, bias, window, name):
    _, dil, _, length, _ = zr.shape
    lc = min(length, DILATED_CHUNK_ROWS if dil > 1 else Q_BLOCK * UNITS_PER_STEP)
    nres = min(dil, UNITS_PER_STEP // (lc // Q_BLOCK))
    return _attn_call((zr, 0), (zr, 1), (zr, 2), bias, None, halo=window // (2 * dil), lc=lc,
                      nres=nres, want_lse=True, name=name)


def _rms(x, g):
    return x * lax.rsqrt(jnp.mean(x * x, axis=-1, keepdims=True) + RMS_EPS) * g


def _tokens(src_ref, stage_ref, tm):
    dil = src_ref.shape[0]
    if dil == 1:
        return src_ref[0].astype(_F32)
    for r in range(dil):
        for k in range(QUAD // LANES):
            stage_ref[k, pl.ds(r, tm // dil, stride=dil), :] = (
                src_ref[r, :, k * LANES:(k + 1) * LANES].astype(_F32))
    return jnp.concatenate([stage_ref[k] for k in range(QUAD // LANES)], axis=-1)


def _tail_kernel(x_ref, oa_ref, ob1_ref, ob2_ref, ob3_ref, l1_ref, l2_ref, l3_ref,
                 gx_ref, wg_ref, bg_ref, wa_ref, wb_ref, wo_ref, gm_ref, wu_ref, wd_ref, gf_ref, y_ref,
                 so2_ref, so3_ref, sl2_ref, sl3_ref, *, tm):
    outs = [_tokens(ob1_ref, None, tm), _tokens(ob2_ref, so2_ref, tm), _tokens(ob3_ref, so3_ref, tm)]
    lses = [_tokens(l1_ref, None, tm), _tokens(l2_ref, sl2_ref, tm), _tokens(l3_ref, sl3_ref, tm)]
    top = jnp.maximum(jnp.maximum(lses[0], lses[1]), lses[2])
    ws = [jnp.exp(l - top) for l in lses]
    mix = ws[0] * outs[0] + ws[1] * outs[1] + ws[2] * outs[2]
    o_b = (mix / (ws[0] + ws[1] + ws[2])).astype(_BF16)

    def mixer(rows):
        x = x_ref[rows, :]
        zg = jnp.dot(_rms(x, gx_ref[...]).astype(_BF16), wg_ref[...],
                     preferred_element_type=_F32) + bg_ref[...]
        gates = 0.5 * jnp.tanh(0.5 * zg) + 0.5
        pa = jnp.dot(oa_ref[rows, :], wa_ref[...], preferred_element_type=_F32)
        pb = jnp.dot(o_b[rows], wb_ref[...], preferred_element_type=_F32)
        merged = (gates[:, :D_MODEL] * pa + gates[:, D_MODEL:] * pb).astype(_BF16)
        x1 = x + jnp.dot(merged, wo_ref[...], preferred_element_type=_F32)
        return x1, _rms(x1, gm_ref[...]).astype(_BF16)

    def mlp(rows, x1, h):
        acc = x1
        for c in range(D_FF // D_MODEL):
            cols = slice(c * D_MODEL, (c + 1) * D_MODEL)
            u = jnp.maximum(jnp.dot(h, wu_ref[:, cols], preferred_element_type=_F32), 0.0)
            acc = acc + jnp.dot((u * u).astype(_BF16), wd_ref[cols, :], preferred_element_type=_F32)
        y_ref[rows, :] = _rms(acc, gf_ref[...])

    halves = [slice(i * (tm // 2), (i + 1) * (tm // 2)) for i in range(2)]
    mixed = [mixer(rows) for rows in halves]
    for rows, (x1, h) in zip(halves, mixed):
        mlp(rows, x1, h)


def _tail(x, o_a, o_bs, lses, g_mix, wg, b_gate, wa, wb, wo, g_mlp, wu, wd, g_final, tm):
    b, s, _ = x.shape
    const = lambda bi, i: (0, 0)
    row = lambda bi, i: (bi, i, 0)
    resident = lambda a: pl.BlockSpec(a.shape, const, pipeline_mode=pl.Buffered(1))
    res_specs = [pl.BlockSpec((None, dil, tm // dil, QUAD), lambda bi, i: (bi, 0, i, 0))
                 for _, dil in B_GROUPS]
    in_specs = (
        [pl.BlockSpec((None, tm, D_MODEL), row), pl.BlockSpec((None, tm, A_Q), row)]
        + res_specs + res_specs
        + [resident(g_mix), resident(wg), resident(b_gate), resident(wa), resident(wb), resident(wo),
           resident(g_mlp), resident(wu), resident(wd), resident(g_final)]
    )
    stage = pltpu.VMEM((QUAD // LANES, tm, LANES), _F32)
    return pl.pallas_call(
        functools.partial(_tail_kernel, tm=tm),
        grid=(b, s // tm),
        in_specs=in_specs,
        out_specs=pl.BlockSpec((None, tm, D_MODEL), row),
        out_shape=jax.ShapeDtypeStruct((b, s, D_MODEL), _F32),
        scratch_shapes=[stage] * 4,
        compiler_params=pltpu.CompilerParams(
            dimension_semantics=("arbitrary", "arbitrary"), vmem_limit_bytes=VMEM_LIMIT_BYTES),
        name="tail",
    )(x, o_a, *o_bs, *lses, g_mix, wg, b_gate, wa, wb, wo, g_mlp, wu, wd, g_final)


def _prepare(rel_bias, g_mix, w_in, b_gate, w_branch_a, w_branch_b, w_out, attn_sink, g_mlp,
             w_up, w_down, g_final):
    scale = HEAD_DIM ** -0.5 * LOG2E
    order = np.asarray(A_HEAD_ORDER)
    o1, o2, o3 = A_Q, A_Q + A_KV, A_Q + 2 * A_KV
    o4, o5 = o3 + B_W, o3 + 2 * B_W
    o6 = o3 + 3 * B_W
    qa = w_in[:, :o1].reshape(D_MODEL, A_HEADS, HEAD_DIM)[:, order].reshape(D_MODEL, A_Q) * scale
    slabs = [qa, w_in[:, o1:o2], w_in[:, o2:o3]]
    for gi in range(len(B_GROUPS)):
        cols = slice(gi * B_OUT, (gi + 1) * B_OUT)
        slabs += [w_in[:, o3:o4][:, cols] * scale, w_in[:, o4:o5][:, cols], w_in[:, o5:o6][:, cols]]
    w_cat = jnp.concatenate(slabs, axis=1).astype(_BF16)

    wa = w_branch_a.reshape(A_HEADS, HEAD_DIM, D_MODEL)[order].reshape(A_Q, D_MODEL).astype(_BF16)
    bias_a = _bias_blocks(rel_bias[:, :A_HEADS][:, order], A_WINDOW, 1, Q_BLOCK + 2 * A_WINDOW)
    bias_b = []
    for gi, (window, dil) in enumerate(B_GROUPS):
        n = window // (2 * dil)
        table = rel_bias[:, A_HEADS + gi * B_HEADS_PER_GROUP:A_HEADS + (gi + 1) * B_HEADS_PER_GROUP]
        bias_b.append(_bias_blocks(table, n, dil, Q_BLOCK + 2 * n))
    return dict(
        w_cat=w_cat, wg=w_in[:, o6:].astype(_BF16), g_mix=g_mix.reshape(1, D_MODEL),
        b_gate=b_gate.reshape(1, 2 * D_MODEL),
        wa=wa, wb=w_branch_b.astype(_BF16), wo=w_out.astype(_BF16),
        sink=attn_sink[order].astype(_F32) * LOG2E, g_mlp=g_mlp.reshape(1, D_MODEL),
        wu=w_up.astype(_BF16), wd=w_down.astype(_BF16), g_final=g_final.reshape(1, D_MODEL),
        bias_a=bias_a, bias_b=bias_b,
    )


def _encode(x, p):
    qa, ka, va, zb1, zb2, zb3 = _inproj(x, p["g_mix"], p["w_cat"], tm=INPROJ_ROWS)
    o_a = _attn_a(qa, ka, va, p["bias_a"], p["sink"])
    o_bs, lses = [], []
    for gi, ((window, _), zr) in enumerate(zip(B_GROUPS, (zb1, zb2, zb3))):
        o, lse = _attn_b(zr, p["bias_b"][gi], window, f"attn_b{gi}")
        o_bs.append(o)
        lses.append(lse)
    return _tail(x, o_a, o_bs, lses, p["g_mix"], p["wg"], p["b_gate"], p["wa"], p["wb"], p["wo"],
                 p["g_mlp"], p["wu"], p["wd"], p["g_final"], tm=TAIL_ROWS)


def kernel(x_prompt, x_sample, rel_bias, g_mix, w_in, b_gate, w_branch_a, w_branch_b, w_out,
           attn_sink, g_mlp, w_up, w_down, g_final):
    assert g_mix.shape[0] == 1, "single-layer encoder"
    p = _prepare(rel_bias, g_mix[0], w_in[0], b_gate[0], w_branch_a[0], w_branch_b[0], w_out[0],
                 attn_sink[0], g_mlp[0], w_up[0], w_down[0], g_final)
    return (_encode(x_prompt, p), _encode(x_sample, p))
```

```python
import functools
import math

import numpy as np
import jax
import jax.numpy as jnp
from jax import lax
from jax.experimental import pallas as pl
from jax.experimental.pallas import tpu as pltpu

D_MODEL = 1024
HEAD_DIM = 64
A_HEADS = 8
A_KV_HEADS = 2
A_WINDOW = 128
B_GROUPS = ((128, 1), (512, 4), (2048, 16))
B_HEADS_PER_GROUP = 4
B_HEADS = B_HEADS_PER_GROUP * len(B_GROUPS)
REL_BUCKETS = 32
REL_MAX_DISTANCE = 1024
D_FF = 4 * D_MODEL
A_Q = A_HEADS * HEAD_DIM
A_KV = A_KV_HEADS * HEAD_DIM
B_W = B_HEADS * HEAD_DIM
B_OUT = B_HEADS_PER_GROUP * HEAD_DIM
RMS_EPS = 1e-6
NEG_INF = -1e30
LOG2E = math.log2(math.e)
LN2 = math.log(2.0)

LANES = 128
QUAD = 4 * HEAD_DIM
PAIR = 2 * HEAD_DIM
SLAB = 3 * QUAD
Q_BLOCK = 128
FOLD = Q_BLOCK // 2
INPROJ_ROWS = 1024
SUB_ROWS = 256
TAIL_ROWS = 512
UNITS_PER_STEP = 32
DILATED_CHUNK_ROWS = 1024
VMEM_LIMIT_BYTES = 58 * 1024 * 1024

A_HEAD_ORDER = (0, 4, 1, 5, 2, 6, 3, 7)

_BF16 = jnp.bfloat16
_F32 = jnp.float32


def _rel_bucket(rel):
    half = REL_BUCKETS // 2
    max_exact = half // 2
    n = np.abs(rel)
    large = max_exact + (np.log(np.maximum(n, 1) / max_exact)
                         / np.log(REL_MAX_DISTANCE / max_exact) * (half - max_exact)).astype(np.int32)
    large = np.minimum(large, half - 1)
    return (np.where(rel > 0, half, 0) + np.where(n < max_exact, n, large)).astype(np.int32)


def _bias_blocks(table, n, dist_scale, nk):
    rel = np.arange(nk)[:, None] - n - np.arange(Q_BLOCK)[None, :]
    bucket = _rel_bucket(rel * dist_scale)
    table = table.astype(_F32) * LOG2E
    bias = jnp.zeros((table.shape[1], nk, Q_BLOCK), _F32)
    for bkt in np.unique(bucket):
        bias = jnp.where(jnp.asarray(bucket == bkt)[None], table[bkt][:, None, None], bias)
    band = np.abs(rel) <= n
    key = np.arange(nk)[:, None]
    variants = []
    for var in range(4):
        ok = band.copy()
        if var & 1:
            ok &= key >= n
        if var & 2:
            ok &= key < nk - n
        variants.append(jnp.where(ok[None], bias, NEG_INF))
    tiles = jnp.stack(variants)
    return jnp.concatenate(
        [jnp.where(np.arange(Q_BLOCK) < FOLD, tiles[..., :FOLD, :], tiles[..., nk - FOLD:, :]),
         tiles[..., FOLD:nk - FOLD, :]], axis=-2)


def _inproj_kernel(x_ref, g_ref, w_ref, qa_ref, ka_ref, va_ref, zb1_ref, zb2_ref, zb3_ref, zs_ref,
                   *, tm):
    x = x_ref[...]
    ms = jnp.mean(x * x, axis=-1, keepdims=True)
    h = (x * lax.rsqrt(ms + RMS_EPS) * g_ref[...]).astype(_BF16)
    for sub in range(tm // SUB_ROWS):
        rows = slice(sub * SUB_ROWS, (sub + 1) * SUB_ROWS)
        hs = h[rows]
        for gi in reversed(range(len(B_GROUPS))):
            out, dil = (zb1_ref, zb2_ref, zb3_ref)[gi], B_GROUPS[gi][1]
            z = jnp.dot(hs, w_ref[:, (gi + 1) * SLAB:(gi + 2) * SLAB], preferred_element_type=_F32)
            if dil == 1:
                for part in range(3):
                    out[0, part, rows, :] = z[:, part * QUAD:(part + 1) * QUAD].astype(_BF16)
                continue
            for k in range(SLAB // LANES):
                zs_ref[k, rows, :] = z[:, k * LANES:(k + 1) * LANES]
            per = SUB_ROWS // dil
            for r in range(dil):
                for k in range(SLAB // LANES):
                    part, half = divmod(k, QUAD // LANES)
                    out[r, part, sub * per:(sub + 1) * per, half * LANES:(half + 1) * LANES] = (
                        zs_ref[k, pl.ds(sub * SUB_ROWS + r, per, stride=dil), :].astype(_BF16))
        za = jnp.dot(hs, w_ref[:, :SLAB], preferred_element_type=_F32).astype(_BF16)
        qa_ref[rows, :] = za[:, :A_Q]
        ka_ref[rows, :] = za[:, A_Q:A_Q + A_KV]
        va_ref[rows, :] = za[:, A_Q + A_KV:]


def _inproj(x, g_mix, w_cat, tm):
    b, s, _ = x.shape
    const = lambda bi, i: (0, 0)
    row = lambda bi, i: (bi, i, 0)
    res_specs = [pl.BlockSpec((None, dil, 3, tm // dil, QUAD), lambda bi, i: (bi, 0, 0, i, 0))
                 for _, dil in B_GROUPS]
    res_shapes = [jax.ShapeDtypeStruct((b, dil, 3, s // dil, QUAD), _BF16) for _, dil in B_GROUPS]
    a_widths = (A_Q, A_KV, A_KV)
    return pl.pallas_call(
        functools.partial(_inproj_kernel, tm=tm),
        grid=(b, s // tm),
        in_specs=[
            pl.BlockSpec((None, tm, D_MODEL), row),
            pl.BlockSpec((1, D_MODEL), const),
            pl.BlockSpec(w_cat.shape, const, pipeline_mode=pl.Buffered(1)),
        ],
        out_specs=[pl.BlockSpec((None, tm, w), row) for w in a_widths] + res_specs,
        out_shape=[jax.ShapeDtypeStruct((b, s, w), _BF16) for w in a_widths] + res_shapes,
        scratch_shapes=[pltpu.VMEM((SLAB // LANES, tm, LANES), _F32)],
        compiler_params=pltpu.CompilerParams(
            dimension_semantics=("arbitrary", "arbitrary"), vmem_limit_bytes=VMEM_LIMIT_BYTES),
        name="inproj",
    )(x, g_mix, w_cat)


def _attn_ops(q_ref, kv_refs, bias_ref, sink_ref, stage_refs, write, *, halo, lc, kv_pairs,
              first, last, want_lse):
    kc_ref, vc_ref, kp_ref, vp_ref, kn_ref, vn_ref = kv_refs
    kext_ref, vext_ref, vt_ref = stage_refs
    nk = Q_BLOCK + 2 * halo
    nblk = lc // Q_BLOCK
    qgrp = lax.broadcasted_iota(jnp.int32, (Q_BLOCK, QUAD), 1) // HEAD_DIM
    low_queries = lax.broadcasted_iota(jnp.int32, (FOLD, Q_BLOCK), 1) < FOLD

    def stage(r):
        pieces = ((0, halo, kp_ref, vp_ref), (halo, lc, kc_ref, vc_ref),
                  (halo + lc, halo, kn_ref, vn_ref))
        for lo, rows, k_src, v_src in pieces:
            k = k_src[r]
            kext_ref[r, lo:lo + rows, :] = k if kv_pairs == 2 else jnp.concatenate([k, k], axis=-1)
            vext_ref[r, lo:lo + rows, :] = v_src[r]
        vt_ref[r] = vext_ref[r].astype(_F32).T.astype(_BF16)

    def scores(r, i, quad):
        if i == 0 and quad == 0:
            stage(r)
        r0 = i * Q_BLOCK
        q4 = q_ref[r, r0:r0 + Q_BLOCK, quad * QUAD:(quad + 1) * QUAD]
        qs = jnp.concatenate(
            [jnp.where(qgrp == j, q4, jnp.zeros_like(q4)) for j in range(4)], axis=0)
        return lax.dot_general(kext_ref[r, r0:r0 + nk, :], qs, (((1,), (1,)), ((), ())),
                               preferred_element_type=_F32)

    def finish(r, i, quad, st):
        r0 = i * Q_BLOCK
        var = (first if i == 0 else 0) + (last if i == nblk - 1 else 0)
        pts, scales, lses = [], [], []
        for j in range(4):
            head = quad * 4 + j
            s = st[:, j * Q_BLOCK:(j + 1) * Q_BLOCK]
            s = jnp.concatenate(
                [jnp.where(low_queries, s[:FOLD], s[nk - FOLD:]), s[FOLD:nk - FOLD]], axis=0)
            logits = s + bias_ref[var, head]
            m = jnp.max(logits, axis=0, keepdims=True)
            if sink_ref is not None:
                sink = sink_ref[head]
                m = jnp.maximum(m, sink)
            p = jnp.exp2(logits - m)
            den = jnp.sum(p, axis=0, keepdims=True)
            if sink_ref is not None:
                den = den + jnp.exp2(sink - m)
            shared, zero = p[:FOLD], jnp.zeros((FOLD, Q_BLOCK), _F32)
            p = jnp.concatenate([jnp.where(low_queries, shared, zero), p[FOLD:],
                                 jnp.where(low_queries, zero, shared)], axis=0)
            pts.append(p.astype(_BF16))
            scales.append(1.0 / den)
            lses.append(m * LN2 + jnp.log(den))
        outs = []
        for pair in range(2):
            j0 = 2 * pair
            vrows = (pair % kv_pairs) * PAIR
            ot = jnp.dot(vt_ref[r, vrows:vrows + PAIR, r0:r0 + nk],
                         jnp.concatenate(pts[j0:j0 + 2], axis=1), preferred_element_type=_F32)
            outs.append(ot[:HEAD_DIM, :Q_BLOCK] * scales[j0])
            outs.append(ot[HEAD_DIM:, Q_BLOCK:] * scales[j0 + 1])
        lse = None
        if want_lse:
            lse = jnp.concatenate(
                [jnp.broadcast_to(l, (HEAD_DIM, Q_BLOCK)) for l in lses], axis=0).T
        write(r, r0, quad, jnp.concatenate(outs, axis=0).T.astype(_BF16), lse)

    return scores, finish


def _attn_kernel(*refs, halo, lc, nres, nquad, kv_pairs, has_sink, want_lse, chunk_axis):
    it = iter(refs)
    q_ref = next(it)
    kv_refs = [next(it) for _ in range(6)]
    bias_ref = next(it)
    sink_ref = next(it) if has_sink else None
    o_ref = next(it)
    lse_ref = next(it) if want_lse else None
    stage_refs = [next(it) for _ in range(3)]

    c = pl.program_id(chunk_axis)
    first = jnp.where(c == 0, 1, 0)
    last = jnp.where(c == pl.num_programs(chunk_axis) - 1, 2, 0)

    def write(r, r0, quad, o, lse):
        o_ref[r, r0:r0 + Q_BLOCK, quad * QUAD:(quad + 1) * QUAD] = o
        if want_lse:
            lse_ref[r, r0:r0 + Q_BLOCK, :] = lse

    scores, finish = _attn_ops(q_ref, kv_refs, bias_ref, sink_ref, stage_refs, write, halo=halo,
                               lc=lc, kv_pairs=kv_pairs, first=first, last=last, want_lse=want_lse)

    units = [(r, i, quad) for r in range(nres) for i in range(lc // Q_BLOCK) for quad in range(nquad)]
    ahead = 2 if nquad == 1 else 1
    pending = [scores(*u) for u in units[:ahead]]
    for idx, unit in enumerate(units):
        if idx + ahead < len(units):
            pending.append(scores(*units[idx + ahead]))
        finish(*unit, pending.pop(0))


def _attn_call(q, k, v, bias, sink, *, halo, lc, nres, want_lse, name):
    def operand(x):
        arr, part = x if isinstance(x, tuple) else (x, None)
        width = arr.shape[-1]

        def spec(rows, row_block):
            if part is None:
                return pl.BlockSpec((None, nres, rows, width),
                                    lambda bi, r, c: (bi, r, row_block(c), 0))
            return pl.BlockSpec((None, nres, None, rows, width),
                                lambda bi, r, c: (bi, r, part, row_block(c), 0))
        return arr, spec

    (q_arr, q_spec), (k_arr, k_spec), (v_arr, v_spec) = operand(q), operand(k), operand(v)
    b, dil, length, qw, kvw = q_arr.shape[0], q_arr.shape[1], q_arr.shape[-2], q_arr.shape[-1], k_arr.shape[-1]
    per_chunk = lc // halo
    ext = lc + 2 * halo
    here = lambda c: c
    prev = lambda c: jnp.maximum(c * per_chunk - 1, 0)
    nxt = lambda c: jnp.minimum((c + 1) * per_chunk, length // halo - 1)
    in_specs = [
        q_spec(lc, here), k_spec(lc, here), v_spec(lc, here),
        k_spec(halo, prev), v_spec(halo, prev), k_spec(halo, nxt), v_spec(halo, nxt),
        pl.BlockSpec(bias.shape, lambda bi, r, c: (0, 0, 0, 0), pipeline_mode=pl.Buffered(1)),
    ]
    args = [q_arr, k_arr, v_arr, k_arr, v_arr, k_arr, v_arr, bias]
    if sink is not None:
        in_specs.append(pl.BlockSpec(memory_space=pltpu.SMEM))
        args.append(sink)
    out_spec = pl.BlockSpec((None, nres, lc, qw), lambda bi, r, c: (bi, r, c, 0))
    out_specs, out_shape = [out_spec], [jax.ShapeDtypeStruct((b, dil, length, qw), _BF16)]
    if want_lse:
        out_specs.append(out_spec)
        out_shape.append(jax.ShapeDtypeStruct((b, dil, length, qw), _F32))
    kernel = functools.partial(
        _attn_kernel, halo=halo, lc=lc, nres=nres, nquad=qw // QUAD, kv_pairs=kvw // PAIR,
        has_sink=sink is not None, want_lse=want_lse, chunk_axis=2)
    grid = (b, dil // nres, length // lc)
    return pl.pallas_call(
        kernel,
        grid=grid,
        in_specs=in_specs,
        out_specs=out_specs,
        out_shape=out_shape,
        scratch_shapes=[
            pltpu.VMEM((nres, ext, QUAD), _BF16),
            pltpu.VMEM((nres, ext, kvw), _BF16),
            pltpu.VMEM((nres, kvw, ext), _BF16),
        ],
        compiler_params=pltpu.CompilerParams(
            dimension_semantics=("arbitrary",) * len(grid), vmem_limit_bytes=VMEM_LIMIT_BYTES),
        name=name,
    )(*args)


def _attn_a(qa, ka, va, bias, sink):
    b, s, _ = qa.shape
    lc = min(s, Q_BLOCK * UNITS_PER_STEP // (A_Q // QUAD))
    seq = lambda a: a.reshape(b, 1, s, a.shape[-1])
    (o,) = _attn_call(seq(qa), seq(ka), seq(va), bias, sink, halo=A_WINDOW, lc=lc, nres=1,
                      want_lse=False, name="attn_a")
    return o.reshape(b, s, A_Q)


def _attn_b(zr, bias, window, name):
    _, dil, _, length, _ = zr.shape
    lc = min(length, DILATED_CHUNK_ROWS if dil > 1 else Q_BLOCK * UNITS_PER_STEP)
    nres = min(dil, UNITS_PER_STEP // (lc // Q_BLOCK))
    return _attn_call((zr, 0), (zr, 1), (zr, 2), bias, None, halo=window // (2 * dil), lc=lc,
                      nres=nres, want_lse=True, name=name)


def _rms(x, g):
    return x * lax.rsqrt(jnp.mean(x * x, axis=-1, keepdims=True) + RMS_EPS) * g


def _tokens(src_ref, stage_ref, tm):
    dil = src_ref.shape[0]
    if dil == 1:
        return src_ref[0].astype(_F32)
    for r in range(dil):
        for k in range(QUAD // LANES):
            stage_ref[k, pl.ds(r, tm // dil, stride=dil), :] = (
                src_ref[r, :, k * LANES:(k + 1) * LANES].astype(_F32))
    return jnp.concatenate([stage_ref[k] for k in range(QUAD // LANES)], axis=-1)


def _tail_kernel(x_ref, oa_ref, ob1_ref, ob2_ref, ob3_ref, l1_ref, l2_ref, l3_ref,
                 gx_ref, wg_ref, bg_ref, wa_ref, wb_ref, wo_ref, gm_ref, wu_ref, wd_ref, gf_ref, y_ref,
                 so2_ref, so3_ref, sl2_ref, sl3_ref, *, tm):
    outs = [_tokens(ob1_ref, None, tm), _tokens(ob2_ref, so2_ref, tm), _tokens(ob3_ref, so3_ref, tm)]
    lses = [_tokens(l1_ref, None, tm), _tokens(l2_ref, sl2_ref, tm), _tokens(l3_ref, sl3_ref, tm)]
    top = jnp.maximum(jnp.maximum(lses[0], lses[1]), lses[2])
    ws = [jnp.exp(l - top) for l in lses]
    mix = ws[0] * outs[0] + ws[1] * outs[1] + ws[2] * outs[2]
    o_b = (mix / (ws[0] + ws[1] + ws[2])).astype(_BF16)

    def branches(rows):
        pa = jnp.dot(oa_ref[rows, :], wa_ref[...], preferred_element_type=_F32)
        pb = jnp.dot(o_b[rows], wb_ref[...], preferred_element_type=_F32)
        zg = jnp.dot(_rms(x_ref[rows, :], gx_ref[...]).astype(_BF16), wg_ref[...],
                     preferred_element_type=_F32) + bg_ref[...]
        gates = 0.5 * jnp.tanh(0.5 * zg) + 0.5
        return (gates[:, :D_MODEL] * pa + gates[:, D_MODEL:] * pb).astype(_BF16)

    def residual(rows, merged):
        x1 = x_ref[rows, :] + jnp.dot(merged, wo_ref[...], preferred_element_type=_F32)
        return x1, _rms(x1, gm_ref[...]).astype(_BF16)

    def mlp(rows, x1, h):
        acc = x1
        for c in range(D_FF // D_MODEL):
            cols = slice(c * D_MODEL, (c + 1) * D_MODEL)
            u = jnp.maximum(jnp.dot(h, wu_ref[:, cols], preferred_element_type=_F32), 0.0)
            acc = acc + jnp.dot((u * u).astype(_BF16), wd_ref[cols, :], preferred_element_type=_F32)
        y_ref[rows, :] = _rms(acc, gf_ref[...])

    halves = [slice(i * (tm // 2), (i + 1) * (tm // 2)) for i in range(2)]
    merged = [branches(rows) for rows in halves]
    mixed = [residual(rows, m) for rows, m in zip(halves, merged)]
    for rows, (x1, h) in zip(halves, mixed):
        mlp(rows, x1, h)


def _tail(x, o_a, o_bs, lses, g_mix, wg, b_gate, wa, wb, wo, g_mlp, wu, wd, g_final, tm):
    b, s, _ = x.shape
    const = lambda bi, i: (0, 0)
    row = lambda bi, i: (bi, i, 0)
    resident = lambda a: pl.BlockSpec(a.shape, const, pipeline_mode=pl.Buffered(1))
    res_specs = [pl.BlockSpec((None, dil, tm // dil, QUAD), lambda bi, i: (bi, 0, i, 0))
                 for _, dil in B_GROUPS]
    in_specs = (
        [pl.BlockSpec((None, tm, D_MODEL), row), pl.BlockSpec((None, tm, A_Q), row)]
        + res_specs + res_specs
        + [resident(g_mix), resident(wg), resident(b_gate), resident(wa), resident(wb), resident(wo),
           resident(g_mlp), resident(wu), resident(wd), resident(g_final)]
    )
    stage = pltpu.VMEM((QUAD // LANES, tm, LANES), _F32)
    return pl.pallas_call(
        functools.partial(_tail_kernel, tm=tm),
        grid=(b, s // tm),
        in_specs=in_specs,
        out_specs=pl.BlockSpec((None, tm, D_MODEL), row),
        out_shape=jax.ShapeDtypeStruct((b, s, D_MODEL), _F32),
        scratch_shapes=[stage] * 4,
        compiler_params=pltpu.CompilerParams(
            dimension_semantics=("arbitrary", "arbitrary"), vmem_limit_bytes=VMEM_LIMIT_BYTES),
        name="tail",
    )(x, o_a, *o_bs, *lses, g_mix, wg, b_gate, wa, wb, wo, g_mlp, wu, wd, g_final)


def _prepare(rel_bias, g_mix, w_in, b_gate, w_branch_a, w_branch_b, w_out, attn_sink, g_mlp,
             w_up, w_down, g_final):
    scale = HEAD_DIM ** -0.5 * LOG2E
    order = np.asarray(A_HEAD_ORDER)
    o1, o2, o3 = A_Q, A_Q + A_KV, A_Q + 2 * A_KV
    o4, o5 = o3 + B_W, o3 + 2 * B_W
    o6 = o3 + 3 * B_W
    qa = w_in[:, :o1].reshape(D_MODEL, A_HEADS, HEAD_DIM)[:, order].reshape(D_MODEL, A_Q) * scale
    slabs = [qa, w_in[:, o1:o2], w_in[:, o2:o3]]
    for gi in range(len(B_GROUPS)):
        cols = slice(gi * B_OUT, (gi + 1) * B_OUT)
        slabs += [w_in[:, o3:o4][:, cols] * scale, w_in[:, o4:o5][:, cols], w_in[:, o5:o6][:, cols]]
    w_cat = jnp.concatenate(slabs, axis=1).astype(_BF16)

    wa = w_branch_a.reshape(A_HEADS, HEAD_DIM, D_MODEL)[order].reshape(A_Q, D_MODEL).astype(_BF16)
    bias_a = _bias_blocks(rel_bias[:, :A_HEADS][:, order], A_WINDOW, 1, Q_BLOCK + 2 * A_WINDOW)
    bias_b = []
    for gi, (window, dil) in enumerate(B_GROUPS):
        n = window // (2 * dil)
        table = rel_bias[:, A_HEADS + gi * B_HEADS_PER_GROUP:A_HEADS + (gi + 1) * B_HEADS_PER_GROUP]
        bias_b.append(_bias_blocks(table, n, dil, Q_BLOCK + 2 * n))
    return dict(
        w_cat=w_cat, wg=w_in[:, o6:].astype(_BF16), g_mix=g_mix.reshape(1, D_MODEL),
        b_gate=b_gate.reshape(1, 2 * D_MODEL),
        wa=wa, wb=w_branch_b.astype(_BF16), wo=w_out.astype(_BF16),
        sink=attn_sink[order].astype(_F32) * LOG2E, g_mlp=g_mlp.reshape(1, D_MODEL),
        wu=w_up.astype(_BF16), wd=w_down.astype(_BF16), g_final=g_final.reshape(1, D_MODEL),
        bias_a=bias_a, bias_b=bias_b,
    )


def _encode(x, p):
    qa, ka, va, zb1, zb2, zb3 = _inproj(x, p["g_mix"], p["w_cat"], tm=INPROJ_ROWS)
    o_a = _attn_a(qa, ka, va, p["bias_a"], p["sink"])
    o_bs, lses = [], []
    for gi, ((window, _), zr) in enumerate(zip(B_GROUPS, (zb1, zb2, zb3))):
        o, lse = _attn_b(zr, p["bias_b"][gi], window, f"attn_b{gi}")
        o_bs.append(o)
        lses.append(lse)
    return _tail(x, o_a, o_bs, lses, p["g_mix"], p["wg"], p["b_gate"], p["wa"], p["wb"], p["wo"],
                 p["g_mlp"], p["wu"], p["wd"], p["g_final"], tm=TAIL_ROWS)


def kernel(x_prompt, x_sample, rel_bias, g_mix, w_in, b_gate, w_branch_a, w_branch_b, w_out,
           attn_sink, g_mlp, w_up, w_down, g_final):
    assert g_mix.shape[0] == 1, "single-layer encoder"
    p = _prepare(rel_bias, g_mix[0], w_in[0], b_gate[0], w_branch_a[0], w_branch_b[0], w_out[0],
                 attn_sink[0], g_mlp[0], w_up[0], w_down[0], g_final)
    return (_encode(x_prompt, p), _encode(x_sample, p))
```

```python
import functools
import math

import numpy as np
import jax
import jax.numpy as jnp
from jax import lax
from jax.experimental import pallas as pl
from jax.experimental.pallas import tpu as pltpu

D_MODEL = 1024
HEAD_DIM = 64
A_HEADS = 8
A_KV_HEADS = 2
A_WINDOW = 128
B_GROUPS = ((128, 1), (512, 4), (2048, 16))
B_HEADS_PER_GROUP = 4
B_HEADS = B_HEADS_PER_GROUP * len(B_GROUPS)
REL_BUCKETS = 32
REL_MAX_DISTANCE = 1024
D_FF = 4 * D_MODEL
A_Q = A_HEADS * HEAD_DIM
A_KV = A_KV_HEADS * HEAD_DIM
B_W = B_HEADS * HEAD_DIM
B_OUT = B_HEADS_PER_GROUP * HEAD_DIM
RMS_EPS = 1e-6
NEG_INF = -1e30
LOG2E = math.log2(math.e)
LN2 = math.log(2.0)

LANES = 128
QUAD = 4 * HEAD_DIM
PAIR = 2 * HEAD_DIM
SLAB = 3 * QUAD
Q_BLOCK = 128
FOLD = Q_BLOCK // 2
INPROJ_ROWS = 1024
SUB_ROWS = 256
RELAYOUT_STRIDE = 4
TAIL_ROWS = 512
UNITS_PER_STEP = 32
DILATED_CHUNK_ROWS = 1024
VMEM_LIMIT_BYTES = 58 * 1024 * 1024

A_HEAD_ORDER = (0, 4, 1, 5, 2, 6, 3, 7)

_BF16 = jnp.bfloat16
_F32 = jnp.float32


def _rel_bucket(rel):
    half = REL_BUCKETS // 2
    max_exact = half // 2
    n = np.abs(rel)
    large = max_exact + (np.log(np.maximum(n, 1) / max_exact)
                         / np.log(REL_MAX_DISTANCE / max_exact) * (half - max_exact)).astype(np.int32)
    large = np.minimum(large, half - 1)
    return (np.where(rel > 0, half, 0) + np.where(n < max_exact, n, large)).astype(np.int32)


def _bias_blocks(table, n, dist_scale, nk):
    rel = np.arange(nk)[:, None] - n - np.arange(Q_BLOCK)[None, :]
    bucket = _rel_bucket(rel * dist_scale)
    table = table.astype(_F32) * LOG2E
    bias = jnp.zeros((table.shape[1], nk, Q_BLOCK), _F32)
    for bkt in np.unique(bucket):
        bias = jnp.where(jnp.asarray(bucket == bkt)[None], table[bkt][:, None, None], bias)
    band = np.abs(rel) <= n
    key = np.arange(nk)[:, None]
    variants = []
    for var in range(4):
        ok = band.copy()
        if var & 1:
            ok &= key >= n
        if var & 2:
            ok &= key < nk - n
        variants.append(jnp.where(ok[None], bias, NEG_INF))
    tiles = jnp.stack(variants)
    return jnp.concatenate(
        [jnp.where(np.arange(Q_BLOCK) < FOLD, tiles[..., :FOLD, :], tiles[..., nk - FOLD:, :]),
         tiles[..., FOLD:nk - FOLD, :]], axis=-2)


def _inproj_kernel(x_ref, g_ref, w_ref, qa_ref, ka_ref, va_ref, zb1_ref, zb2_ref, zb3_ref, zs_ref,
                   zt_ref, *, tm):
    x = x_ref[...]
    ms = jnp.mean(x * x, axis=-1, keepdims=True)
    h = (x * lax.rsqrt(ms + RMS_EPS) * g_ref[...]).astype(_BF16)
    for sub in range(tm // SUB_ROWS):
        rows = slice(sub * SUB_ROWS, (sub + 1) * SUB_ROWS)
        hs = h[rows]
        for gi in reversed(range(len(B_GROUPS))):
            out, dil = (zb1_ref, zb2_ref, zb3_ref)[gi], B_GROUPS[gi][1]
            z = jnp.dot(hs, w_ref[:, (gi + 1) * SLAB:(gi + 2) * SLAB], preferred_element_type=_F32)
            if dil == 1:
                for part in range(3):
                    out[0, part, rows, :] = z[:, part * QUAD:(part + 1) * QUAD].astype(_BF16)
                continue
            for k in range(SLAB // LANES):
                zs_ref[k, rows, :] = z[:, k * LANES:(k + 1) * LANES]
            src_ref, base, stride, per = zs_ref, sub * SUB_ROWS, dil, SUB_ROWS // dil
            if dil == RELAYOUT_STRIDE ** 2:
                quarter, stride = SUB_ROWS // RELAYOUT_STRIDE, RELAYOUT_STRIDE
                for k in range(SLAB // LANES):
                    for r in range(stride):
                        zt_ref[k, r * quarter:(r + 1) * quarter, :] = (
                            zs_ref[k, pl.ds(base + r, quarter, stride=stride), :])
                src_ref, base = zt_ref, 0
            for r in range(dil):
                start = base + r if stride == dil else (r % stride) * quarter + r // stride
                for k in range(SLAB // LANES):
                    part, half = divmod(k, QUAD // LANES)
                    out[r, part, sub * per:(sub + 1) * per, half * LANES:(half + 1) * LANES] = (
                        src_ref[k, pl.ds(start, per, stride=stride), :].astype(_BF16))
        za = jnp.dot(hs, w_ref[:, :SLAB], preferred_element_type=_F32).astype(_BF16)
        qa_ref[rows, :] = za[:, :A_Q]
        ka_ref[rows, :] = za[:, A_Q:A_Q + A_KV]
        va_ref[rows, :] = za[:, A_Q + A_KV:]


def _inproj(x, g_mix, w_cat, tm):
    b, s, _ = x.shape
    const = lambda bi, i: (0, 0)
    row = lambda bi, i: (bi, i, 0)
    res_specs = [pl.BlockSpec((None, dil, 3, tm // dil, QUAD), lambda bi, i: (bi, 0, 0, i, 0))
                 for _, dil in B_GROUPS]
    res_shapes = [jax.ShapeDtypeStruct((b, dil, 3, s // dil, QUAD), _BF16) for _, dil in B_GROUPS]
    a_widths = (A_Q, A_KV, A_KV)
    return pl.pallas_call(
        functools.partial(_inproj_kernel, tm=tm),
        grid=(b, s // tm),
        in_specs=[
            pl.BlockSpec((None, tm, D_MODEL), row),
            pl.BlockSpec((1, D_MODEL), const),
            pl.BlockSpec(w_cat.shape, const, pipeline_mode=pl.Buffered(1)),
        ],
        out_specs=[pl.BlockSpec((None, tm, w), row) for w in a_widths] + res_specs,
        out_shape=[jax.ShapeDtypeStruct((b, s, w), _BF16) for w in a_widths] + res_shapes,
        scratch_shapes=[pltpu.VMEM((SLAB // LANES, tm, LANES), _F32),
                        pltpu.VMEM((SLAB // LANES, SUB_ROWS, LANES), _F32)],
        compiler_params=pltpu.CompilerParams(
            dimension_semantics=("arbitrary", "arbitrary"), vmem_limit_bytes=VMEM_LIMIT_BYTES),
        name="inproj",
    )(x, g_mix, w_cat)


def _attn_ops(q_ref, kv_refs, bias_ref, sink_ref, stage_refs, write, *, halo, lc, kv_pairs,
              first, last, want_lse):
    kc_ref, vc_ref, kp_ref, vp_ref, kn_ref, vn_ref = kv_refs
    kext_ref, vext_ref, vt_ref = stage_refs
    nk = Q_BLOCK + 2 * halo
    nblk = lc // Q_BLOCK
    qgrp = lax.broadcasted_iota(jnp.int32, (Q_BLOCK, QUAD), 1) // HEAD_DIM
    low_queries = lax.broadcasted_iota(jnp.int32, (FOLD, Q_BLOCK), 1) < FOLD

    def stage(r):
        pieces = ((0, halo, kp_ref, vp_ref), (halo, lc, kc_ref, vc_ref),
                  (halo + lc, halo, kn_ref, vn_ref))
        for lo, rows, k_src, v_src in pieces:
            k = k_src[r]
            kext_ref[r, lo:lo + rows, :] = k if kv_pairs == 2 else jnp.concatenate([k, k], axis=-1)
            vext_ref[r, lo:lo + rows, :] = v_src[r]
        vt_ref[r] = vext_ref[r].astype(_F32).T.astype(_BF16)

    def scores(r, i, quad):
        if i == 0 and quad == 0:
            stage(r)
        r0 = i * Q_BLOCK
        q4 = q_ref[r, r0:r0 + Q_BLOCK, quad * QUAD:(quad + 1) * QUAD]
        qs = jnp.concatenate(
            [jnp.where(qgrp == j, q4, jnp.zeros_like(q4)) for j in range(4)], axis=0)
        return lax.dot_general(kext_ref[r, r0:r0 + nk, :], qs, (((1,), (1,)), ((), ())),
                               preferred_element_type=_F32)

    def finish(r, i, quad, st):
        r0 = i * Q_BLOCK
        var = (first if i == 0 else 0) + (last if i == nblk - 1 else 0)
        pts, scales, lses = [], [], []
        for j in range(4):
            head = quad * 4 + j
            s = st[:, j * Q_BLOCK:(j + 1) * Q_BLOCK]
            s = jnp.concatenate(
                [jnp.where(low_queries, s[:FOLD], s[nk - FOLD:]), s[FOLD:nk - FOLD]], axis=0)
            logits = s + bias_ref[var, head]
            m = jnp.max(logits, axis=0, keepdims=True)
            if sink_ref is not None:
                sink = sink_ref[head]
                m = jnp.maximum(m, sink)
            p = jnp.exp2(logits - m)
            den = jnp.sum(p, axis=0, keepdims=True)
            if sink_ref is not None:
                den = den + jnp.exp2(sink - m)
            shared, zero = p[:FOLD], jnp.zeros((FOLD, Q_BLOCK), _F32)
            p = jnp.concatenate([jnp.where(low_queries, shared, zero), p[FOLD:],
                                 jnp.where(low_queries, zero, shared)], axis=0)
            pts.append(p.astype(_BF16))
            scales.append(1.0 / den)
            lses.append(m * LN2 + jnp.log(den))
        outs = []
        for pair in range(2):
            j0 = 2 * pair
            vrows = (pair % kv_pairs) * PAIR
            ot = jnp.dot(vt_ref[r, vrows:vrows + PAIR, r0:r0 + nk],
                         jnp.concatenate(pts[j0:j0 + 2], axis=1), preferred_element_type=_F32)
            outs.append(ot[:HEAD_DIM, :Q_BLOCK] * scales[j0])
            outs.append(ot[HEAD_DIM:, Q_BLOCK:] * scales[j0 + 1])
        lse = None
        if want_lse:
            lse = jnp.concatenate(
                [jnp.broadcast_to(l, (HEAD_DIM, Q_BLOCK)) for l in lses], axis=0).T
        write(r, r0, quad, jnp.concatenate(outs, axis=0).T.astype(_BF16), lse)

    return scores, finish


def _attn_kernel(*refs, halo, lc, nres, nquad, kv_pairs, has_sink, want_lse, chunk_axis):
    it = iter(refs)
    q_ref = next(it)
    kv_refs = [next(it) for _ in range(6)]
    bias_ref = next(it)
    sink_ref = next(it) if has_sink else None
    o_ref = next(it)
    lse_ref = next(it) if want_lse else None
    stage_refs = [next(it) for _ in range(3)]

    c = pl.program_id(chunk_axis)
    first = jnp.where(c == 0, 1, 0)
    last = jnp.where(c == pl.num_programs(chunk_axis) - 1, 2, 0)

    def write(r, r0, quad, o, lse):
        o_ref[r, r0:r0 + Q_BLOCK, quad * QUAD:(quad + 1) * QUAD] = o
        if want_lse:
            lse_ref[r, r0:r0 + Q_BLOCK, :] = lse

    scores, finish = _attn_ops(q_ref, kv_refs, bias_ref, sink_ref, stage_refs, write, halo=halo,
                               lc=lc, kv_pairs=kv_pairs, first=first, last=last, want_lse=want_lse)

    units = [(r, i, quad) for r in range(nres) for i in range(lc // Q_BLOCK) for quad in range(nquad)]
    ahead = 2 if nquad == 1 else 1
    pending = [scores(*u) for u in units[:ahead]]
    for idx, unit in enumerate(units):
        if idx + ahead < len(units):
            pending.append(scores(*units[idx + ahead]))
        finish(*unit, pending.pop(0))


def _attn_call(q, k, v, bias, sink, *, halo, lc, nres, want_lse, name):
    def operand(x):
        arr, part = x if isinstance(x, tuple) else (x, None)
        width = arr.shape[-1]

        def spec(rows, row_block):
            if part is None:
                return pl.BlockSpec((None, nres, rows, width),
                                    lambda bi, r, c: (bi, r, row_block(c), 0))
            return pl.BlockSpec((None, nres, None, rows, width),
                                lambda bi, r, c: (bi, r, part, row_block(c), 0))
        return arr, spec

    (q_arr, q_spec), (k_arr, k_spec), (v_arr, v_spec) = operand(q), operand(k), operand(v)
    b, dil, length, qw, kvw = q_arr.shape[0], q_arr.shape[1], q_arr.shape[-2], q_arr.shape[-1], k_arr.shape[-1]
    per_chunk = lc // halo
    ext = lc + 2 * halo
    here = lambda c: c
    prev = lambda c: jnp.maximum(c * per_chunk - 1, 0)
    nxt = lambda c: jnp.minimum((c + 1) * per_chunk, length // halo - 1)
    in_specs = [
        q_spec(lc, here), k_spec(lc, here), v_spec(lc, here),
        k_spec(halo, prev), v_spec(halo, prev), k_spec(halo, nxt), v_spec(halo, nxt),
        pl.BlockSpec(bias.shape, lambda bi, r, c: (0, 0, 0, 0), pipeline_mode=pl.Buffered(1)),
    ]
    args = [q_arr, k_arr, v_arr, k_arr, v_arr, k_arr, v_arr, bias]
    if sink is not None:
        in_specs.append(pl.BlockSpec(memory_space=pltpu.SMEM))
        args.append(sink)
    out_spec = pl.BlockSpec((None, nres, lc, qw), lambda bi, r, c: (bi, r, c, 0))
    out_specs, out_shape = [out_spec], [jax.ShapeDtypeStruct((b, dil, length, qw), _BF16)]
    if want_lse:
        out_specs.append(out_spec)
        out_shape.append(jax.ShapeDtypeStruct((b, dil, length, qw), _F32))
    kernel = functools.partial(
        _attn_kernel, halo=halo, lc=lc, nres=nres, nquad=qw // QUAD, kv_pairs=kvw // PAIR,
        has_sink=sink is not None, want_lse=want_lse, chunk_axis=2)
    grid = (b, dil // nres, length // lc)
    return pl.pallas_call(
        kernel,
        grid=grid,
        in_specs=in_specs,
        out_specs=out_specs,
        out_shape=out_shape,
        scratch_shapes=[
            pltpu.VMEM((nres, ext, QUAD), _BF16),
            pltpu.VMEM((nres, ext, kvw), _BF16),
            pltpu.VMEM((nres, kvw, ext), _BF16),
        ],
        compiler_params=pltpu.CompilerParams(
            dimension_semantics=("arbitrary",) * len(grid), vmem_limit_bytes=VMEM_LIMIT_BYTES),
        name=name,
    )(*args)


def _attn_a(qa, ka, va, bias, sink):
    b, s, _ = qa.shape
    lc = min(s, Q_BLOCK * UNITS_PER_STEP // (A_Q // QUAD))
    seq = lambda a: a.reshape(b, 1, s, a.shape[-1])
    (o,) = _attn_call(seq(qa), seq(ka), seq(va), bias, sink, halo=A_WINDOW, lc=lc, nres=1,
                      want_lse=False, name="attn_a")
    return o.reshape(b, s, A_Q)


def _attn_b(zr, bias, window, name):
    _, dil, _, length, _ = zr.shape
    lc = min(length, DILATED_CHUNK_ROWS if dil > 1 else Q_BLOCK * UNITS_PER_STEP)
    nres = min(dil, UNITS_PER_STEP // (lc // Q_BLOCK))
    return _attn_call((zr, 0), (zr, 1), (zr, 2), bias, None, halo=window // (2 * dil), lc=lc,
                      nres=nres, want_lse=True, name=name)


def _rms(x, g):
    return x * lax.rsqrt(jnp.mean(x * x, axis=-1, keepdims=True) + RMS_EPS) * g


def _tokens(src_ref, stage_ref, tm):
    dil = src_ref.shape[0]
    if dil == 1:
        return src_ref[0].astype(_F32)
    for r in range(dil):
        for k in range(QUAD // LANES):
            stage_ref[k, pl.ds(r, tm // dil, stride=dil), :] = (
                src_ref[r, :, k * LANES:(k + 1) * LANES].astype(_F32))
    return jnp.concatenate([stage_ref[k] for k in range(QUAD // LANES)], axis=-1)


def _tail_kernel(x_ref, oa_ref, ob1_ref, ob2_ref, ob3_ref, l1_ref, l2_ref, l3_ref,
                 gx_ref, wg_ref, bg_ref, wa_ref, wb_ref, wo_ref, gm_ref, wu_ref, wd_ref, gf_ref, y_ref,
                 so2_ref, so3_ref, sl2_ref, sl3_ref, *, tm):
    outs = [_tokens(ob1_ref, None, tm), _tokens(ob2_ref, so2_ref, tm), _tokens(ob3_ref, so3_ref, tm)]
    lses = [_tokens(l1_ref, None, tm), _tokens(l2_ref, sl2_ref, tm), _tokens(l3_ref, sl3_ref, tm)]
    top = jnp.maximum(jnp.maximum(lses[0], lses[1]), lses[2])
    ws = [jnp.exp(l - top) for l in lses]
    mix = ws[0] * outs[0] + ws[1] * outs[1] + ws[2] * outs[2]
    o_b = (mix / (ws[0] + ws[1] + ws[2])).astype(_BF16)

    def branches(rows):
        pa = jnp.dot(oa_ref[rows, :], wa_ref[...], preferred_element_type=_F32)
        pb = jnp.dot(o_b[rows], wb_ref[...], preferred_element_type=_F32)
        zg = jnp.dot(_rms(x_ref[rows, :], gx_ref[...]).astype(_BF16), wg_ref[...],
                     preferred_element_type=_F32) + bg_ref[...]
        gates = 0.5 * jnp.tanh(0.5 * zg) + 0.5
        return (gates[:, :D_MODEL] * pa + gates[:, D_MODEL:] * pb).astype(_BF16)

    def residual(rows, merged):
        x1 = x_ref[rows, :] + jnp.dot(merged, wo_ref[...], preferred_element_type=_F32)
        return x1, _rms(x1, gm_ref[...]).astype(_BF16)

    def mlp(rows, x1, h):
        acc = x1
        for c in range(D_FF // D_MODEL):
            cols = slice(c * D_MODEL, (c + 1) * D_MODEL)
            u = jnp.maximum(jnp.dot(h, wu_ref[:, cols], preferred_element_type=_F32), 0.0)
            acc = acc + jnp.dot((u * u).astype(_BF16), wd_ref[cols, :], preferred_element_type=_F32)
        y_ref[rows, :] = _rms(acc, gf_ref[...])

    halves = [slice(i * (tm // 2), (i + 1) * (tm // 2)) for i in range(2)]
    merged = [branches(rows) for rows in halves]
    mixed = [residual(rows, m) for rows, m in zip(halves, merged)]
    for rows, (x1, h) in zip(halves, mixed):
        mlp(rows, x1, h)


def _tail(x, o_a, o_bs, lses, g_mix, wg, b_gate, wa, wb, wo, g_mlp, wu, wd, g_final, tm):
    b, s, _ = x.shape
    const = lambda bi, i: (0, 0)
    row = lambda bi, i: (bi, i, 0)
    resident = lambda a: pl.BlockSpec(a.shape, const, pipeline_mode=pl.Buffered(1))
    res_specs = [pl.BlockSpec((None, dil, tm // dil, QUAD), lambda bi, i: (bi, 0, i, 0))
                 for _, dil in B_GROUPS]
    in_specs = (
        [pl.BlockSpec((None, tm, D_MODEL), row), pl.BlockSpec((None, tm, A_Q), row)]
        + res_specs + res_specs
        + [resident(g_mix), resident(wg), resident(b_gate), resident(wa), resident(wb), resident(wo),
           resident(g_mlp), resident(wu), resident(wd), resident(g_final)]
    )
    stage = pltpu.VMEM((QUAD // LANES, tm, LANES), _F32)
    return pl.pallas_call(
        functools.partial(_tail_kernel, tm=tm),
        grid=(b, s // tm),
        in_specs=in_specs,
        out_specs=pl.BlockSpec((None, tm, D_MODEL), row),
        out_shape=jax.ShapeDtypeStruct((b, s, D_MODEL), _F32),
        scratch_shapes=[stage] * 4,
        compiler_params=pltpu.CompilerParams(
            dimension_semantics=("arbitrary", "arbitrary"), vmem_limit_bytes=VMEM_LIMIT_BYTES),
        name="tail",
    )(x, o_a, *o_bs, *lses, g_mix, wg, b_gate, wa, wb, wo, g_mlp, wu, wd, g_final)


def _prepare(rel_bias, g_mix, w_in, b_gate, w_branch_a, w_branch_b, w_out, attn_sink, g_mlp,
             w_up, w_down, g_final):
    scale = HEAD_DIM ** -0.5 * LOG2E
    order = np.asarray(A_HEAD_ORDER)
    o1, o2, o3 = A_Q, A_Q + A_KV, A_Q + 2 * A_KV
    o4, o5 = o3 + B_W, o3 + 2 * B_W
    o6 = o3 + 3 * B_W
    qa = w_in[:, :o1].reshape(D_MODEL, A_HEADS, HEAD_DIM)[:, order].reshape(D_MODEL, A_Q) * scale
    slabs = [qa, w_in[:, o1:o2], w_in[:, o2:o3]]
    for gi in range(len(B_GROUPS)):
        cols = slice(gi * B_OUT, (gi + 1) * B_OUT)
        slabs += [w_in[:, o3:o4][:, cols] * scale, w_in[:, o4:o5][:, cols], w_in[:, o5:o6][:, cols]]
    w_cat = jnp.concatenate(slabs, axis=1).astype(_BF16)

    wa = w_branch_a.reshape(A_HEADS, HEAD_DIM, D_MODEL)[order].reshape(A_Q, D_MODEL).astype(_BF16)
    bias_a = _bias_blocks(rel_bias[:, :A_HEADS][:, order], A_WINDOW, 1, Q_BLOCK + 2 * A_WINDOW)
    bias_b = []
    for gi, (window, dil) in enumerate(B_GROUPS):
        n = window // (2 * dil)
        table = rel_bias[:, A_HEADS + gi * B_HEADS_PER_GROUP:A_HEADS + (gi + 1) * B_HEADS_PER_GROUP]
        bias_b.append(_bias_blocks(table, n, dil, Q_BLOCK + 2 * n))
    return dict(
        w_cat=w_cat, wg=w_in[:, o6:].astype(_BF16), g_mix=g_mix.reshape(1, D_MODEL),
        b_gate=b_gate.reshape(1, 2 * D_MODEL),
        wa=wa, wb=w_branch_b.astype(_BF16), wo=w_out.astype(_BF16),
        sink=attn_sink[order].astype(_F32) * LOG2E, g_mlp=g_mlp.reshape(1, D_MODEL),
        wu=w_up.astype(_BF16), wd=w_down.astype(_BF16), g_final=g_final.reshape(1, D_MODEL),
        bias_a=bias_a, bias_b=bias_b,
    )


def _encode(x, p):
    qa, ka, va, zb1, zb2, zb3 = _inproj(x, p["g_mix"], p["w_cat"], tm=INPROJ_ROWS)
    o_a = _attn_a(qa, ka, va, p["bias_a"], p["sink"])
    o_bs, lses = [], []
    for gi, ((window, _), zr) in enumerate(zip(B_GROUPS, (zb1, zb2, zb3))):
        o, lse = _attn_b(zr, p["bias_b"][gi], window, f"attn_b{gi}")
        o_bs.append(o)
        lses.append(lse)
    return _tail(x, o_a, o_bs, lses, p["g_mix"], p["wg"], p["b_gate"], p["wa"], p["wb"], p["wo"],
                 p["g_mlp"], p["wu"], p["wd"], p["g_final"], tm=TAIL_ROWS)


def kernel(x_prompt, x_sample, rel_bias, g_mix, w_in, b_gate, w_branch_a, w_branch_b, w_out,
           attn_sink, g_mlp, w_up, w_down, g_final):
    assert g_mix.shape[0] == 1, "single-layer encoder"
    p = _prepare(rel_bias, g_mix[0], w_in[0], b_gate[0], w_branch_a[0], w_branch_b[0], w_out[0],
                 attn_sink[0], g_mlp[0], w_up[0], w_down[0], g_final)
    return (_encode(x_prompt, p), _encode(x_sample, p))
```

```python
import functools
import math

import numpy as np
import jax
import jax.numpy as jnp
from jax import lax
from jax.experimental import pallas as pl
from jax.experimental.pallas import tpu as pltpu

D_MODEL = 1024
HEAD_DIM = 64
A_HEADS = 8
A_KV_HEADS = 2
A_WINDOW = 128
B_GROUPS = ((128, 1), (512, 4), (2048, 16))
B_HEADS_PER_GROUP = 4
B_HEADS = B_HEADS_PER_GROUP * len(B_GROUPS)
REL_BUCKETS = 32
REL_MAX_DISTANCE = 1024
D_FF = 4 * D_MODEL
A_Q = A_HEADS * HEAD_DIM
A_KV = A_KV_HEADS * HEAD_DIM
B_W = B_HEADS * HEAD_DIM
B_OUT = B_HEADS_PER_GROUP * HEAD_DIM
RMS_EPS = 1e-6
NEG_INF = -1e30
LOG2E = math.log2(math.e)
LN2 = math.log(2.0)

LANES = 128
QUAD = 4 * HEAD_DIM
PAIR = 2 * HEAD_DIM
SLAB = 3 * QUAD
Q_BLOCK = 128
FOLD = Q_BLOCK // 2
INPROJ_ROWS = 1024
SUB_ROWS = 256
RELAYOUT_STRIDE = 4
TAIL_ROWS = 512
UNITS_PER_STEP = 32
DILATED_CHUNK_ROWS = 1024
VMEM_LIMIT_BYTES = 58 * 1024 * 1024

A_HEAD_ORDER = (0, 4, 1, 5, 2, 6, 3, 7)

_BF16 = jnp.bfloat16
_F32 = jnp.float32


def _rel_bucket(rel):
    half = REL_BUCKETS // 2
    max_exact = half // 2
    n = np.abs(rel)
    large = max_exact + (np.log(np.maximum(n, 1) / max_exact)
                         / np.log(REL_MAX_DISTANCE / max_exact) * (half - max_exact)).astype(np.int32)
    large = np.minimum(large, half - 1)
    return (np.where(rel > 0, half, 0) + np.where(n < max_exact, n, large)).astype(np.int32)


def _bias_blocks(table, n, dist_scale, nk):
    rel = np.arange(nk)[:, None] - n - np.arange(Q_BLOCK)[None, :]
    bucket = _rel_bucket(rel * dist_scale)
    table = table.astype(_F32) * LOG2E
    bias = jnp.zeros((table.shape[1], nk, Q_BLOCK), _F32)
    for bkt in np.unique(bucket):
        bias = jnp.where(jnp.asarray(bucket == bkt)[None], table[bkt][:, None, None], bias)
    band = np.abs(rel) <= n
    key = np.arange(nk)[:, None]
    variants = []
    for var in range(4):
        ok = band.copy()
        if var & 1:
            ok &= key >= n
        if var & 2:
            ok &= key < nk - n
        variants.append(jnp.where(ok[None], bias, NEG_INF))
    tiles = jnp.stack(variants)
    return jnp.concatenate(
        [jnp.where(np.arange(Q_BLOCK) < FOLD, tiles[..., :FOLD, :], tiles[..., nk - FOLD:, :]),
         tiles[..., FOLD:nk - FOLD, :]], axis=-2)


def _inproj_kernel(x_ref, g_ref, w_ref, qa_ref, ka_ref, va_ref, zb1_ref, zb2_ref, zb3_ref, zs_ref,
                   zt_ref, *, tm):
    x = x_ref[...]
    ms = jnp.mean(x * x, axis=-1, keepdims=True)
    h = (x * lax.rsqrt(ms + RMS_EPS) * g_ref[...]).astype(_BF16)
    for sub in range(tm // SUB_ROWS):
        rows = slice(sub * SUB_ROWS, (sub + 1) * SUB_ROWS)
        hs = h[rows]
        for gi in reversed(range(len(B_GROUPS))):
            out, dil = (zb1_ref, zb2_ref, zb3_ref)[gi], B_GROUPS[gi][1]
            z = jnp.dot(hs, w_ref[:, (gi + 1) * SLAB:(gi + 2) * SLAB], preferred_element_type=_F32)
            if dil == 1:
                for part in range(3):
                    out[0, part, rows, :] = z[:, part * QUAD:(part + 1) * QUAD].astype(_BF16)
                continue
            for k in range(SLAB // LANES):
                zs_ref[k, rows, :] = z[:, k * LANES:(k + 1) * LANES]
            src_ref, base, stride, per = zs_ref, sub * SUB_ROWS, dil, SUB_ROWS // dil
            if dil == RELAYOUT_STRIDE ** 2:
                quarter, stride = SUB_ROWS // RELAYOUT_STRIDE, RELAYOUT_STRIDE
                for k in range(SLAB // LANES):
                    for r in range(stride):
                        zt_ref[k, r * quarter:(r + 1) * quarter, :] = (
                            zs_ref[k, pl.ds(base + r, quarter, stride=stride), :])
                src_ref, base = zt_ref, 0
            for r in range(dil):
                start = base + r if stride == dil else (r % stride) * quarter + r // stride
                for k in range(SLAB // LANES):
                    part, half = divmod(k, QUAD // LANES)
                    out[r, part, sub * per:(sub + 1) * per, half * LANES:(half + 1) * LANES] = (
                        src_ref[k, pl.ds(start, per, stride=stride), :].astype(_BF16))
        za = jnp.dot(hs, w_ref[:, :SLAB], preferred_element_type=_F32).astype(_BF16)
        qa_ref[rows, :] = za[:, :A_Q]
        ka_ref[rows, :] = za[:, A_Q:A_Q + A_KV]
        va_ref[rows, :] = za[:, A_Q + A_KV:]


def _inproj(x, g_mix, w_cat, tm):
    b, s, _ = x.shape
    const = lambda bi, i: (0, 0)
    row = lambda bi, i: (bi, i, 0)
    res_specs = [pl.BlockSpec((None, dil, 3, tm // dil, QUAD), lambda bi, i: (bi, 0, 0, i, 0))
                 for _, dil in B_GROUPS]
    res_shapes = [jax.ShapeDtypeStruct((b, dil, 3, s // dil, QUAD), _BF16) for _, dil in B_GROUPS]
    a_widths = (A_Q, A_KV, A_KV)
    return pl.pallas_call(
        functools.partial(_inproj_kernel, tm=tm),
        grid=(b, s // tm),
        in_specs=[
            pl.BlockSpec((None, tm, D_MODEL), row),
            pl.BlockSpec((1, D_MODEL), const),
            pl.BlockSpec(w_cat.shape, const, pipeline_mode=pl.Buffered(1)),
        ],
        out_specs=[pl.BlockSpec((None, tm, w), row) for w in a_widths] + res_specs,
        out_shape=[jax.ShapeDtypeStruct((b, s, w), _BF16) for w in a_widths] + res_shapes,
        scratch_shapes=[pltpu.VMEM((SLAB // LANES, tm, LANES), _F32),
                        pltpu.VMEM((SLAB // LANES, SUB_ROWS, LANES), _F32)],
        compiler_params=pltpu.CompilerParams(
            dimension_semantics=("arbitrary", "arbitrary"), vmem_limit_bytes=VMEM_LIMIT_BYTES),
        name="inproj",
    )(x, g_mix, w_cat)


def _attn_ops(q_ref, kv_refs, bias_ref, sink_ref, stage_refs, write, *, halo, lc, kv_pairs,
              first, last, want_lse):
    kc_ref, vc_ref, kp_ref, vp_ref, kn_ref, vn_ref = kv_refs
    kext_ref, vext_ref, vt_ref = stage_refs
    nk = Q_BLOCK + 2 * halo
    nblk = lc // Q_BLOCK
    qgrp = lax.broadcasted_iota(jnp.int32, (Q_BLOCK, QUAD), 1) // HEAD_DIM
    low_queries = lax.broadcasted_iota(jnp.int32, (FOLD, Q_BLOCK), 1) < FOLD

    def stage(r):
        pieces = ((0, halo, kp_ref, vp_ref), (halo, lc, kc_ref, vc_ref),
                  (halo + lc, halo, kn_ref, vn_ref))
        for lo, rows, k_src, v_src in pieces:
            k = k_src[r]
            kext_ref[r, lo:lo + rows, :] = k if kv_pairs == 2 else jnp.concatenate([k, k], axis=-1)
            vext_ref[r, lo:lo + rows, :] = v_src[r]
        vt_ref[r] = vext_ref[r].astype(_F32).T.astype(_BF16)

    def scores(r, i, quad):
        if i == 0 and quad == 0:
            stage(r)
        r0 = i * Q_BLOCK
        q4 = q_ref[r, r0:r0 + Q_BLOCK, quad * QUAD:(quad + 1) * QUAD]
        qs = jnp.concatenate(
            [jnp.where(qgrp == j, q4, jnp.zeros_like(q4)) for j in range(4)], axis=0)
        return lax.dot_general(kext_ref[r, r0:r0 + nk, :], qs, (((1,), (1,)), ((), ())),
                               preferred_element_type=_F32)

    def finish(r, i, quad, st):
        r0 = i * Q_BLOCK
        var = (first if i == 0 else 0) + (last if i == nblk - 1 else 0)
        pts, scales, lses = [], [], []
        for j in range(4):
            head = quad * 4 + j
            s = st[:, j * Q_BLOCK:(j + 1) * Q_BLOCK]
            s = jnp.concatenate(
                [jnp.where(low_queries, s[:FOLD], s[nk - FOLD:]), s[FOLD:nk - FOLD]], axis=0)
            logits = s + bias_ref[var, head]
            m = jnp.max(logits, axis=0, keepdims=True)
            if sink_ref is not None:
                sink = sink_ref[head]
                m = jnp.maximum(m, sink)
            p = jnp.exp2(logits - m)
            den = jnp.sum(p, axis=0, keepdims=True)
            if sink_ref is not None:
                den = den + jnp.exp2(sink - m)
            shared, zero = p[:FOLD], jnp.zeros((FOLD, Q_BLOCK), _F32)
            p = jnp.concatenate([jnp.where(low_queries, shared, zero), p[FOLD:],
                                 jnp.where(low_queries, zero, shared)], axis=0)
            pts.append(p.astype(_BF16))
            scales.append(1.0 / den)
            lses.append(m * LN2 + jnp.log(den))
        outs = []
        for pair in range(2):
            j0 = 2 * pair
            vrows = (pair % kv_pairs) * PAIR
            ot = jnp.dot(vt_ref[r, vrows:vrows + PAIR, r0:r0 + nk],
                         jnp.concatenate(pts[j0:j0 + 2], axis=1), preferred_element_type=_F32)
            outs.append(ot[:HEAD_DIM, :Q_BLOCK] * scales[j0])
            outs.append(ot[HEAD_DIM:, Q_BLOCK:] * scales[j0 + 1])
        lse = None
        if want_lse:
            lse = jnp.concatenate(
                [jnp.broadcast_to(l, (HEAD_DIM, Q_BLOCK)) for l in lses], axis=0).T
        write(r, r0, quad, jnp.concatenate(outs, axis=0).T.astype(_BF16), lse)

    return scores, finish


def _attn_kernel(*refs, halo, lc, nres, nquad, kv_pairs, has_sink, want_lse, chunk_axis):
    it = iter(refs)
    q_ref = next(it)
    kv_refs = [next(it) for _ in range(6)]
    bias_ref = next(it)
    sink_ref = next(it) if has_sink else None
    o_ref = next(it)
    lse_ref = next(it) if want_lse else None
    stage_refs = [next(it) for _ in range(3)]

    c = pl.program_id(chunk_axis)
    first = jnp.where(c == 0, 1, 0)
    last = jnp.where(c == pl.num_programs(chunk_axis) - 1, 2, 0)

    def write(r, r0, quad, o, lse):
        o_ref[r, r0:r0 + Q_BLOCK, quad * QUAD:(quad + 1) * QUAD] = o
        if want_lse:
            lse_ref[r, r0:r0 + Q_BLOCK, :] = lse

    scores, finish = _attn_ops(q_ref, kv_refs, bias_ref, sink_ref, stage_refs, write, halo=halo,
                               lc=lc, kv_pairs=kv_pairs, first=first, last=last, want_lse=want_lse)

    units = [(r, i, quad) for r in range(nres) for i in range(lc // Q_BLOCK) for quad in range(nquad)]
    ahead = 2 if nquad == 1 else 1
    pending = [scores(*u) for u in units[:ahead]]
    for idx, unit in enumerate(units):
        if idx + ahead < len(units):
            pending.append(scores(*units[idx + ahead]))
        finish(*unit, pending.pop(0))


def _attn_call(q, k, v, bias, sink, *, halo, lc, nres, want_lse, name):
    def operand(x):
        arr, part = x if isinstance(x, tuple) else (x, None)
        width = arr.shape[-1]

        def spec(rows, row_block):
            if part is None:
                return pl.BlockSpec((None, nres, rows, width),
                                    lambda bi, r, c: (bi, r, row_block(c), 0))
            return pl.BlockSpec((None, nres, None, rows, width),
                                lambda bi, r, c: (bi, r, part, row_block(c), 0))
        return arr, spec

    (q_arr, q_spec), (k_arr, k_spec), (v_arr, v_spec) = operand(q), operand(k), operand(v)
    b, dil, length, qw, kvw = q_arr.shape[0], q_arr.shape[1], q_arr.shape[-2], q_arr.shape[-1], k_arr.shape[-1]
    per_chunk = lc // halo
    ext = lc + 2 * halo
    here = lambda c: c
    prev = lambda c: jnp.maximum(c * per_chunk - 1, 0)
    nxt = lambda c: jnp.minimum((c + 1) * per_chunk, length // halo - 1)
    in_specs = [
        q_spec(lc, here), k_spec(lc, here), v_spec(lc, here),
        k_spec(halo, prev), v_spec(halo, prev), k_spec(halo, nxt), v_spec(halo, nxt),
        pl.BlockSpec(bias.shape, lambda bi, r, c: (0, 0, 0, 0), pipeline_mode=pl.Buffered(1)),
    ]
    args = [q_arr, k_arr, v_arr, k_arr, v_arr, k_arr, v_arr, bias]
    if sink is not None:
        in_specs.append(pl.BlockSpec(memory_space=pltpu.SMEM))
        args.append(sink)
    out_spec = pl.BlockSpec((None, nres, lc, qw), lambda bi, r, c: (bi, r, c, 0))
    out_specs, out_shape = [out_spec], [jax.ShapeDtypeStruct((b, dil, length, qw), _BF16)]
    if want_lse:
        out_specs.append(out_spec)
        out_shape.append(jax.ShapeDtypeStruct((b, dil, length, qw), _F32))
    kernel = functools.partial(
        _attn_kernel, halo=halo, lc=lc, nres=nres, nquad=qw // QUAD, kv_pairs=kvw // PAIR,
        has_sink=sink is not None, want_lse=want_lse, chunk_axis=2)
    grid = (b, dil // nres, length // lc)
    return pl.pallas_call(
        kernel,
        grid=grid,
        in_specs=in_specs,
        out_specs=out_specs,
        out_shape=out_shape,
        scratch_shapes=[
            pltpu.VMEM((nres, ext, QUAD), _BF16),
            pltpu.VMEM((nres, ext, kvw), _BF16),
            pltpu.VMEM((nres, kvw, ext), _BF16),
        ],
        compiler_params=pltpu.CompilerParams(
            dimension_semantics=("arbitrary",) * len(grid), vmem_limit_bytes=VMEM_LIMIT_BYTES),
        name=name,
    )(*args)


def _attn_a(qa, ka, va, bias, sink):
    b, s, _ = qa.shape
    lc = min(s, Q_BLOCK * UNITS_PER_STEP // (A_Q // QUAD))
    seq = lambda a: a.reshape(b, 1, s, a.shape[-1])
    (o,) = _attn_call(seq(qa), seq(ka), seq(va), bias, sink, halo=A_WINDOW, lc=lc, nres=1,
                      want_lse=False, name="attn_a")
    return o.reshape(b, s, A_Q)


def _attn_b(zr, bias, window, name):
    _, dil, _, length, _ = zr.shape
    lc = min(length, DILATED_CHUNK_ROWS if dil > 1 else Q_BLOCK * UNITS_PER_STEP)
    nres = min(dil, UNITS_PER_STEP // (lc // Q_BLOCK))
    return _attn_call((zr, 0), (zr, 1), (zr, 2), bias, None, halo=window // (2 * dil), lc=lc,
                      nres=nres, want_lse=True, name=name)


def _rms(x, g):
    return x * lax.rsqrt(jnp.mean(x * x, axis=-1, keepdims=True) + RMS_EPS) * g


def _tokens(src_ref, stage_ref, tmp_ref, tm):
    dil = src_ref.shape[0]
    if dil == 1:
        return src_ref[0].astype(_F32)
    lane_slabs = range(QUAD // LANES)
    if dil == RELAYOUT_STRIDE ** 2:
        per, quarter = tm // dil, tm // RELAYOUT_STRIDE
        for r in range(dil):
            start = (r % RELAYOUT_STRIDE) * quarter + r // RELAYOUT_STRIDE
            for k in lane_slabs:
                tmp_ref[k, pl.ds(start, per, stride=RELAYOUT_STRIDE), :] = (
                    src_ref[r, :, k * LANES:(k + 1) * LANES].astype(_F32))
        for r in range(RELAYOUT_STRIDE):
            for k in lane_slabs:
                stage_ref[k, pl.ds(r, quarter, stride=RELAYOUT_STRIDE), :] = (
                    tmp_ref[k, r * quarter:(r + 1) * quarter, :])
    else:
        for r in range(dil):
            for k in lane_slabs:
                stage_ref[k, pl.ds(r, tm // dil, stride=dil), :] = (
                    src_ref[r, :, k * LANES:(k + 1) * LANES].astype(_F32))
    return jnp.concatenate([stage_ref[k] for k in lane_slabs], axis=-1)


def _tail_kernel(x_ref, oa_ref, ob1_ref, ob2_ref, ob3_ref, l1_ref, l2_ref, l3_ref,
                 gx_ref, wg_ref, bg_ref, wa_ref, wb_ref, wo_ref, gm_ref, wu_ref, wd_ref, gf_ref, y_ref,
                 so2_ref, so3_ref, sl2_ref, sl3_ref, to3_ref, tl3_ref, *, tm):
    outs = [_tokens(ob1_ref, None, None, tm), _tokens(ob2_ref, so2_ref, None, tm),
            _tokens(ob3_ref, so3_ref, to3_ref, tm)]
    lses = [_tokens(l1_ref, None, None, tm), _tokens(l2_ref, sl2_ref, None, tm),
            _tokens(l3_ref, sl3_ref, tl3_ref, tm)]
    top = jnp.maximum(jnp.maximum(lses[0], lses[1]), lses[2])
    ws = [jnp.exp(l - top) for l in lses]
    mix = ws[0] * outs[0] + ws[1] * outs[1] + ws[2] * outs[2]
    o_b = (mix / (ws[0] + ws[1] + ws[2])).astype(_BF16)

    def branches(rows):
        pa = jnp.dot(oa_ref[rows, :], wa_ref[...], preferred_element_type=_F32)
        pb = jnp.dot(o_b[rows], wb_ref[...], preferred_element_type=_F32)
        zg = jnp.dot(_rms(x_ref[rows, :], gx_ref[...]).astype(_BF16), wg_ref[...],
                     preferred_element_type=_F32) + bg_ref[...]
        gates = 0.5 * jnp.tanh(0.5 * zg) + 0.5
        return (gates[:, :D_MODEL] * pa + gates[:, D_MODEL:] * pb).astype(_BF16)

    def residual(rows, merged):
        x1 = x_ref[rows, :] + jnp.dot(merged, wo_ref[...], preferred_element_type=_F32)
        return x1, _rms(x1, gm_ref[...]).astype(_BF16)

    def mlp(rows, x1, h):
        acc = x1
        for c in range(D_FF // D_MODEL):
            cols = slice(c * D_MODEL, (c + 1) * D_MODEL)
            u = jnp.maximum(jnp.dot(h, wu_ref[:, cols], preferred_element_type=_F32), 0.0)
            acc = acc + jnp.dot((u * u).astype(_BF16), wd_ref[cols, :], preferred_element_type=_F32)
        y_ref[rows, :] = _rms(acc, gf_ref[...])

    halves = [slice(i * (tm // 2), (i + 1) * (tm // 2)) for i in range(2)]
    merged = [branches(rows) for rows in halves]
    mixed = [residual(rows, m) for rows, m in zip(halves, merged)]
    for rows, (x1, h) in zip(halves, mixed):
        mlp(rows, x1, h)


def _tail(x, o_a, o_bs, lses, g_mix, wg, b_gate, wa, wb, wo, g_mlp, wu, wd, g_final, tm):
    b, s, _ = x.shape
    const = lambda bi, i: (0, 0)
    row = lambda bi, i: (bi, i, 0)
    resident = lambda a: pl.BlockSpec(a.shape, const, pipeline_mode=pl.Buffered(1))
    res_specs = [pl.BlockSpec((None, dil, tm // dil, QUAD), lambda bi, i: (bi, 0, i, 0))
                 for _, dil in B_GROUPS]
    in_specs = (
        [pl.BlockSpec((None, tm, D_MODEL), row), pl.BlockSpec((None, tm, A_Q), row)]
        + res_specs + res_specs
        + [resident(g_mix), resident(wg), resident(b_gate), resident(wa), resident(wb), resident(wo),
           resident(g_mlp), resident(wu), resident(wd), resident(g_final)]
    )
    stage = pltpu.VMEM((QUAD // LANES, tm, LANES), _F32)
    return pl.pallas_call(
        functools.partial(_tail_kernel, tm=tm),
        grid=(b, s // tm),
        in_specs=in_specs,
        out_specs=pl.BlockSpec((None, tm, D_MODEL), row),
        out_shape=jax.ShapeDtypeStruct((b, s, D_MODEL), _F32),
        scratch_shapes=[stage] * 6,
        compiler_params=pltpu.CompilerParams(
            dimension_semantics=("arbitrary", "arbitrary"), vmem_limit_bytes=VMEM_LIMIT_BYTES),
        name="tail",
    )(x, o_a, *o_bs, *lses, g_mix, wg, b_gate, wa, wb, wo, g_mlp, wu, wd, g_final)


def _prepare(rel_bias, g_mix, w_in, b_gate, w_branch_a, w_branch_b, w_out, attn_sink, g_mlp,
             w_up, w_down, g_final):
    scale = HEAD_DIM ** -0.5 * LOG2E
    order = np.asarray(A_HEAD_ORDER)
    o1, o2, o3 = A_Q, A_Q + A_KV, A_Q + 2 * A_KV
    o4, o5 = o3 + B_W, o3 + 2 * B_W
    o6 = o3 + 3 * B_W
    qa = w_in[:, :o1].reshape(D_MODEL, A_HEADS, HEAD_DIM)[:, order].reshape(D_MODEL, A_Q) * scale
    slabs = [qa, w_in[:, o1:o2], w_in[:, o2:o3]]
    for gi in range(len(B_GROUPS)):
        cols = slice(gi * B_OUT, (gi + 1) * B_OUT)
        slabs += [w_in[:, o3:o4][:, cols] * scale, w_in[:, o4:o5][:, cols], w_in[:, o5:o6][:, cols]]
    w_cat = jnp.concatenate(slabs, axis=1).astype(_BF16)

    wa = w_branch_a.reshape(A_HEADS, HEAD_DIM, D_MODEL)[order].reshape(A_Q, D_MODEL).astype(_BF16)
    bias_a = _bias_blocks(rel_bias[:, :A_HEADS][:, order], A_WINDOW, 1, Q_BLOCK + 2 * A_WINDOW)
    bias_b = []
    for gi, (window, dil) in enumerate(B_GROUPS):
        n = window // (2 * dil)
        table = rel_bias[:, A_HEADS + gi * B_HEADS_PER_GROUP:A_HEADS + (gi + 1) * B_HEADS_PER_GROUP]
        bias_b.append(_bias_blocks(table, n, dil, Q_BLOCK + 2 * n))
    return dict(
        w_cat=w_cat, wg=w_in[:, o6:].astype(_BF16), g_mix=g_mix.reshape(1, D_MODEL),
        b_gate=b_gate.reshape(1, 2 * D_MODEL),
        wa=wa, wb=w_branch_b.astype(_BF16), wo=w_out.astype(_BF16),
        sink=attn_sink[order].astype(_F32) * LOG2E, g_mlp=g_mlp.reshape(1, D_MODEL),
        wu=w_up.astype(_BF16), wd=w_down.astype(_BF16), g_final=g_final.reshape(1, D_MODEL),
        bias_a=bias_a, bias_b=bias_b,
    )


def _encode(x, p):
    qa, ka, va, zb1, zb2, zb3 = _inproj(x, p["g_mix"], p["w_cat"], tm=INPROJ_ROWS)
    o_a = _attn_a(qa, ka, va, p["bias_a"], p["sink"])
    o_bs, lses = [], []
    for gi, ((window, _), zr) in enumerate(zip(B_GROUPS, (zb1, zb2, zb3))):
        o, lse = _attn_b(zr, p["bias_b"][gi], window, f"attn_b{gi}")
        o_bs.append(o)
        lses.append(lse)
    return _tail(x, o_a, o_bs, lses, p["g_mix"], p["wg"], p["b_gate"], p["wa"], p["wb"], p["wo"],
                 p["g_mlp"], p["wu"], p["wd"], p["g_final"], tm=TAIL_ROWS)


def kernel(x_prompt, x_sample, rel_bias, g_mix, w_in, b_gate, w_branch_a, w_branch_b, w_out,
           attn_sink, g_mlp, w_up, w_down, g_final):
    assert g_mix.shape[0] == 1, "single-layer encoder"
    p = _prepare(rel_bias, g_mix[0], w_in[0], b_gate[0], w_branch_a[0], w_branch_b[0], w_out[0],
                 attn_sink[0], g_mlp[0], w_up[0], w_down[0], g_final)
    return (_encode(x_prompt, p), _encode(x_sample, p))
```

```python
import functools
import math

import numpy as np
import jax
import jax.numpy as jnp
from jax import lax
from jax.experimental import pallas as pl
from jax.experimental.pallas import tpu as pltpu

D_MODEL = 1024
HEAD_DIM = 64
A_HEADS = 8
A_KV_HEADS = 2
A_WINDOW = 128
B_GROUPS = ((128, 1), (512, 4), (2048, 16))
B_HEADS_PER_GROUP = 4
B_HEADS = B_HEADS_PER_GROUP * len(B_GROUPS)
REL_BUCKETS = 32
REL_MAX_DISTANCE = 1024
D_FF = 4 * D_MODEL
A_Q = A_HEADS * HEAD_DIM
A_KV = A_KV_HEADS * HEAD_DIM
B_W = B_HEADS * HEAD_DIM
B_OUT = B_HEADS_PER_GROUP * HEAD_DIM
RMS_EPS = 1e-6
NEG_INF = -1e30
LOG2E = math.log2(math.e)
LN2 = math.log(2.0)

LANES = 128
QUAD = 4 * HEAD_DIM
PAIR = 2 * HEAD_DIM
SLAB = 3 * QUAD
Q_BLOCK = 128
FOLD = Q_BLOCK // 2
INPROJ_ROWS = 1024
SUB_ROWS = 256
RELAYOUT_STRIDE = 4
TAIL_ROWS = 512
UNITS_PER_STEP = 32
DILATED_CHUNK_ROWS = 1024
A_CHUNK_ROWS = 4096
VMEM_LIMIT_BYTES = 58 * 1024 * 1024

A_HEAD_ORDER = (0, 4, 1, 5, 2, 6, 3, 7)

_BF16 = jnp.bfloat16
_F32 = jnp.float32


def _rel_bucket(rel):
    half = REL_BUCKETS // 2
    max_exact = half // 2
    n = np.abs(rel)
    large = max_exact + (np.log(np.maximum(n, 1) / max_exact)
                         / np.log(REL_MAX_DISTANCE / max_exact) * (half - max_exact)).astype(np.int32)
    large = np.minimum(large, half - 1)
    return (np.where(rel > 0, half, 0) + np.where(n < max_exact, n, large)).astype(np.int32)


def _bias_blocks(table, n, dist_scale, nk):
    rel = np.arange(nk)[:, None] - n - np.arange(Q_BLOCK)[None, :]
    bucket = _rel_bucket(rel * dist_scale)
    table = table.astype(_F32) * LOG2E
    bias = jnp.zeros((table.shape[1], nk, Q_BLOCK), _F32)
    for bkt in np.unique(bucket):
        bias = jnp.where(jnp.asarray(bucket == bkt)[None], table[bkt][:, None, None], bias)
    band = np.abs(rel) <= n
    key = np.arange(nk)[:, None]
    variants = []
    for var in range(4):
        ok = band.copy()
        if var & 1:
            ok &= key >= n
        if var & 2:
            ok &= key < nk - n
        variants.append(jnp.where(ok[None], bias, NEG_INF))
    tiles = jnp.stack(variants)
    return jnp.concatenate(
        [jnp.where(np.arange(Q_BLOCK) < FOLD, tiles[..., :FOLD, :], tiles[..., nk - FOLD:, :]),
         tiles[..., FOLD:nk - FOLD, :]], axis=-2)


def _inproj_kernel(x_ref, g_ref, w_ref, qa_ref, ka_ref, va_ref, zb1_ref, zb2_ref, zb3_ref, zs_ref,
                   zt_ref, *, tm):
    x = x_ref[...]
    ms = jnp.mean(x * x, axis=-1, keepdims=True)
    h = (x * lax.rsqrt(ms + RMS_EPS) * g_ref[...]).astype(_BF16)
    for sub in range(tm // SUB_ROWS):
        rows = slice(sub * SUB_ROWS, (sub + 1) * SUB_ROWS)
        hs = h[rows]
        for gi in reversed(range(len(B_GROUPS))):
            out, dil = (zb1_ref, zb2_ref, zb3_ref)[gi], B_GROUPS[gi][1]
            z = jnp.dot(hs, w_ref[:, (gi + 1) * SLAB:(gi + 2) * SLAB], preferred_element_type=_F32)
            if dil == 1:
                for part in range(3):
                    out[0, part, rows, :] = z[:, part * QUAD:(part + 1) * QUAD].astype(_BF16)
                continue
            for k in range(SLAB // LANES):
                zs_ref[k, rows, :] = z[:, k * LANES:(k + 1) * LANES]
            src_ref, base, stride, per = zs_ref, sub * SUB_ROWS, dil, SUB_ROWS // dil
            if dil == RELAYOUT_STRIDE ** 2:
                quarter, stride = SUB_ROWS // RELAYOUT_STRIDE, RELAYOUT_STRIDE
                for k in range(SLAB // LANES):
                    for r in range(stride):
                        zt_ref[k, r * quarter:(r + 1) * quarter, :] = (
                            zs_ref[k, pl.ds(base + r, quarter, stride=stride), :])
                src_ref, base = zt_ref, 0
            for r in range(dil):
                start = base + r if stride == dil else (r % stride) * quarter + r // stride
                for k in range(SLAB // LANES):
                    part, half = divmod(k, QUAD // LANES)
                    out[r, part, sub * per:(sub + 1) * per, half * LANES:(half + 1) * LANES] = (
                        src_ref[k, pl.ds(start, per, stride=stride), :].astype(_BF16))
        za = jnp.dot(hs, w_ref[:, :SLAB], preferred_element_type=_F32).astype(_BF16)
        qa_ref[rows, :] = za[:, :A_Q]
        ka_ref[rows, :] = za[:, A_Q:A_Q + A_KV]
        va_ref[rows, :] = za[:, A_Q + A_KV:]


def _inproj(x, g_mix, w_cat, tm):
    b, s, _ = x.shape
    const = lambda bi, i: (0, 0)
    row = lambda bi, i: (bi, i, 0)
    res_specs = [pl.BlockSpec((None, dil, 3, tm // dil, QUAD), lambda bi, i: (bi, 0, 0, i, 0))
                 for _, dil in B_GROUPS]
    res_shapes = [jax.ShapeDtypeStruct((b, dil, 3, s // dil, QUAD), _BF16) for _, dil in B_GROUPS]
    a_widths = (A_Q, A_KV, A_KV)
    return pl.pallas_call(
        functools.partial(_inproj_kernel, tm=tm),
        grid=(b, s // tm),
        in_specs=[
            pl.BlockSpec((None, tm, D_MODEL), row),
            pl.BlockSpec((1, D_MODEL), const),
            pl.BlockSpec(w_cat.shape, const, pipeline_mode=pl.Buffered(1)),
        ],
        out_specs=[pl.BlockSpec((None, tm, w), row) for w in a_widths] + res_specs,
        out_shape=[jax.ShapeDtypeStruct((b, s, w), _BF16) for w in a_widths] + res_shapes,
        scratch_shapes=[pltpu.VMEM((SLAB // LANES, tm, LANES), _F32),
                        pltpu.VMEM((SLAB // LANES, SUB_ROWS, LANES), _F32)],
        compiler_params=pltpu.CompilerParams(
            dimension_semantics=("arbitrary", "arbitrary"), vmem_limit_bytes=VMEM_LIMIT_BYTES),
        name="inproj",
    )(x, g_mix, w_cat)


def _attn_ops(q_ref, kv_refs, bias_ref, sink_ref, stage_refs, write, *, halo, lc, kv_pairs,
              first, last, want_lse):
    kc_ref, vc_ref, kp_ref, vp_ref, kn_ref, vn_ref = kv_refs
    kext_ref, vext_ref, vt_ref = stage_refs
    nk = Q_BLOCK + 2 * halo
    nblk = lc // Q_BLOCK
    qgrp = lax.broadcasted_iota(jnp.int32, (Q_BLOCK, QUAD), 1) // HEAD_DIM
    low_queries = lax.broadcasted_iota(jnp.int32, (FOLD, Q_BLOCK), 1) < FOLD

    def stage(r):
        pieces = ((0, halo, kp_ref, vp_ref), (halo, lc, kc_ref, vc_ref),
                  (halo + lc, halo, kn_ref, vn_ref))
        for lo, rows, k_src, v_src in pieces:
            k = k_src[r]
            kext_ref[r, lo:lo + rows, :] = k if kv_pairs == 2 else jnp.concatenate([k, k], axis=-1)
            vext_ref[r, lo:lo + rows, :] = v_src[r]
        vt_ref[r] = vext_ref[r].astype(_F32).T.astype(_BF16)

    def scores(r, i, quad):
        if i == 0 and quad == 0:
            stage(r)
        r0 = i * Q_BLOCK
        q4 = q_ref[r, r0:r0 + Q_BLOCK, quad * QUAD:(quad + 1) * QUAD]
        qs = jnp.concatenate(
            [jnp.where(qgrp == j, q4, jnp.zeros_like(q4)) for j in range(4)], axis=0)
        return lax.dot_general(kext_ref[r, r0:r0 + nk, :], qs, (((1,), (1,)), ((), ())),
                               preferred_element_type=_F32)

    def finish(r, i, quad, st):
        r0 = i * Q_BLOCK
        var = (first if i == 0 else 0) + (last if i == nblk - 1 else 0)
        pts, scales, lses = [], [], []
        for j in range(4):
            head = quad * 4 + j
            s = st[:, j * Q_BLOCK:(j + 1) * Q_BLOCK]
            s = jnp.concatenate(
                [jnp.where(low_queries, s[:FOLD], s[nk - FOLD:]), s[FOLD:nk - FOLD]], axis=0)
            logits = s + bias_ref[var, head]
            m = jnp.max(logits, axis=0, keepdims=True)
            if sink_ref is not None:
                sink = sink_ref[head]
                m = jnp.maximum(m, sink)
            p = jnp.exp2(logits - m)
            den = jnp.sum(p, axis=0, keepdims=True)
            if sink_ref is not None:
                den = den + jnp.exp2(sink - m)
            shared, zero = p[:FOLD], jnp.zeros((FOLD, Q_BLOCK), _F32)
            p = jnp.concatenate([jnp.where(low_queries, shared, zero), p[FOLD:],
                                 jnp.where(low_queries, zero, shared)], axis=0)
            pts.append(p.astype(_BF16))
            scales.append(1.0 / den)
            lses.append(m * LN2 + jnp.log(den))
        outs = []
        for pair in range(2):
            j0 = 2 * pair
            vrows = (pair % kv_pairs) * PAIR
            ot = jnp.dot(vt_ref[r, vrows:vrows + PAIR, r0:r0 + nk],
                         jnp.concatenate(pts[j0:j0 + 2], axis=1), preferred_element_type=_F32)
            outs.append(ot[:HEAD_DIM, :Q_BLOCK] * scales[j0])
            outs.append(ot[HEAD_DIM:, Q_BLOCK:] * scales[j0 + 1])
        lse = None
        if want_lse:
            lse = jnp.concatenate(
                [jnp.broadcast_to(l, (HEAD_DIM, Q_BLOCK)) for l in lses], axis=0).T
        write(r, r0, quad, jnp.concatenate(outs, axis=0).T.astype(_BF16), lse)

    return scores, finish


def _attn_kernel(*refs, halo, lc, nres, nquad, kv_pairs, has_sink, want_lse, chunk_axis):
    it = iter(refs)
    q_ref = next(it)
    kv_refs = [next(it) for _ in range(6)]
    bias_ref = next(it)
    sink_ref = next(it) if has_sink else None
    o_ref = next(it)
    lse_ref = next(it) if want_lse else None
    stage_refs = [next(it) for _ in range(3)]

    c = pl.program_id(chunk_axis)
    first = jnp.where(c == 0, 1, 0)
    last = jnp.where(c == pl.num_programs(chunk_axis) - 1, 2, 0)

    def write(r, r0, quad, o, lse):
        o_ref[r, r0:r0 + Q_BLOCK, quad * QUAD:(quad + 1) * QUAD] = o
        if want_lse:
            lse_ref[r, r0:r0 + Q_BLOCK, :] = lse

    scores, finish = _attn_ops(q_ref, kv_refs, bias_ref, sink_ref, stage_refs, write, halo=halo,
                               lc=lc, kv_pairs=kv_pairs, first=first, last=last, want_lse=want_lse)

    units = [(r, i, quad) for r in range(nres) for i in range(lc // Q_BLOCK) for quad in range(nquad)]
    ahead = 2 if nquad == 1 else 1
    pending = [scores(*u) for u in units[:ahead]]
    for idx, unit in enumerate(units):
        if idx + ahead < len(units):
            pending.append(scores(*units[idx + ahead]))
        finish(*unit, pending.pop(0))


def _attn_call(q, k, v, bias, sink, *, halo, lc, nres, want_lse, name):
    def operand(x):
        arr, part = x if isinstance(x, tuple) else (x, None)
        width = arr.shape[-1]

        def spec(rows, row_block):
            if part is None:
                return pl.BlockSpec((None, nres, rows, width),
                                    lambda bi, r, c: (bi, r, row_block(c), 0))
            return pl.BlockSpec((None, nres, None, rows, width),
                                lambda bi, r, c: (bi, r, part, row_block(c), 0))
        return arr, spec

    (q_arr, q_spec), (k_arr, k_spec), (v_arr, v_spec) = operand(q), operand(k), operand(v)
    b, dil, length, qw, kvw = q_arr.shape[0], q_arr.shape[1], q_arr.shape[-2], q_arr.shape[-1], k_arr.shape[-1]
    per_chunk = lc // halo
    ext = lc + 2 * halo
    here = lambda c: c
    prev = lambda c: jnp.maximum(c * per_chunk - 1, 0)
    nxt = lambda c: jnp.minimum((c + 1) * per_chunk, length // halo - 1)
    in_specs = [
        q_spec(lc, here), k_spec(lc, here), v_spec(lc, here),
        k_spec(halo, prev), v_spec(halo, prev), k_spec(halo, nxt), v_spec(halo, nxt),
        pl.BlockSpec(bias.shape, lambda bi, r, c: (0, 0, 0, 0), pipeline_mode=pl.Buffered(1)),
    ]
    args = [q_arr, k_arr, v_arr, k_arr, v_arr, k_arr, v_arr, bias]
    if sink is not None:
        in_specs.append(pl.BlockSpec(memory_space=pltpu.SMEM))
        args.append(sink)
    out_spec = pl.BlockSpec((None, nres, lc, qw), lambda bi, r, c: (bi, r, c, 0))
    out_specs, out_shape = [out_spec], [jax.ShapeDtypeStruct((b, dil, length, qw), _BF16)]
    if want_lse:
        out_specs.append(out_spec)
        out_shape.append(jax.ShapeDtypeStruct((b, dil, length, qw), _F32))
    kernel = functools.partial(
        _attn_kernel, halo=halo, lc=lc, nres=nres, nquad=qw // QUAD, kv_pairs=kvw // PAIR,
        has_sink=sink is not None, want_lse=want_lse, chunk_axis=2)
    grid = (b, dil // nres, length // lc)
    return pl.pallas_call(
        kernel,
        grid=grid,
        in_specs=in_specs,
        out_specs=out_specs,
        out_shape=out_shape,
        scratch_shapes=[
            pltpu.VMEM((nres, ext, QUAD), _BF16),
            pltpu.VMEM((nres, ext, kvw), _BF16),
            pltpu.VMEM((nres, kvw, ext), _BF16),
        ],
        compiler_params=pltpu.CompilerParams(
            dimension_semantics=("arbitrary",) * len(grid), vmem_limit_bytes=VMEM_LIMIT_BYTES),
        name=name,
    )(*args)


def _attn_a(qa, ka, va, bias, sink):
    b, s, _ = qa.shape
    lc = min(s, A_CHUNK_ROWS)
    seq = lambda a: a.reshape(b, 1, s, a.shape[-1])
    (o,) = _attn_call(seq(qa), seq(ka), seq(va), bias, sink, halo=A_WINDOW, lc=lc, nres=1,
                      want_lse=False, name="attn_a")
    return o.reshape(b, s, A_Q)


def _attn_b(zr, bias, window, name):
    _, dil, _, length, _ = zr.shape
    lc = min(length, DILATED_CHUNK_ROWS if dil > 1 else Q_BLOCK * UNITS_PER_STEP)
    nres = min(dil, UNITS_PER_STEP // (lc // Q_BLOCK))
    return _attn_call((zr, 0), (zr, 1), (zr, 2), bias, None, halo=window // (2 * dil), lc=lc,
                      nres=nres, want_lse=True, name=name)


def _rms(x, g):
    return x * lax.rsqrt(jnp.mean(x * x, axis=-1, keepdims=True) + RMS_EPS) * g


def _tokens(src_ref, stage_ref, tmp_ref, tm):
    dil = src_ref.shape[0]
    if dil == 1:
        return src_ref[0].astype(_F32)
    lane_slabs = range(QUAD // LANES)
    if dil == RELAYOUT_STRIDE ** 2:
        per, quarter = tm // dil, tm // RELAYOUT_STRIDE
        for r in range(dil):
            start = (r % RELAYOUT_STRIDE) * quarter + r // RELAYOUT_STRIDE
            for k in lane_slabs:
                tmp_ref[k, pl.ds(start, per, stride=RELAYOUT_STRIDE), :] = (
                    src_ref[r, :, k * LANES:(k + 1) * LANES].astype(_F32))
        for r in range(RELAYOUT_STRIDE):
            for k in lane_slabs:
                stage_ref[k, pl.ds(r, quarter, stride=RELAYOUT_STRIDE), :] = (
                    tmp_ref[k, r * quarter:(r + 1) * quarter, :])
    else:
        for r in range(dil):
            for k in lane_slabs:
                stage_ref[k, pl.ds(r, tm // dil, stride=dil), :] = (
                    src_ref[r, :, k * LANES:(k + 1) * LANES].astype(_F32))
    return jnp.concatenate([stage_ref[k] for k in lane_slabs], axis=-1)


def _tail_kernel(x_ref, oa_ref, ob1_ref, ob2_ref, ob3_ref, l1_ref, l2_ref, l3_ref,
                 gx_ref, wg_ref, bg_ref, wa_ref, wb_ref, wo_ref, gm_ref, wu_ref, wd_ref, gf_ref, y_ref,
                 so2_ref, so3_ref, sl2_ref, sl3_ref, to3_ref, tl3_ref, *, tm):
    outs = [_tokens(ob1_ref, None, None, tm), _tokens(ob2_ref, so2_ref, None, tm),
            _tokens(ob3_ref, so3_ref, to3_ref, tm)]
    lses = [_tokens(l1_ref, None, None, tm), _tokens(l2_ref, sl2_ref, None, tm),
            _tokens(l3_ref, sl3_ref, tl3_ref, tm)]
    top = jnp.maximum(jnp.maximum(lses[0], lses[1]), lses[2])
    ws = [jnp.exp(l - top) for l in lses]
    mix = ws[0] * outs[0] + ws[1] * outs[1] + ws[2] * outs[2]
    o_b = (mix / (ws[0] + ws[1] + ws[2])).astype(_BF16)

    def branches(rows):
        pa = jnp.dot(oa_ref[rows, :], wa_ref[...], preferred_element_type=_F32)
        pb = jnp.dot(o_b[rows], wb_ref[...], preferred_element_type=_F32)
        zg = jnp.dot(_rms(x_ref[rows, :], gx_ref[...]).astype(_BF16), wg_ref[...],
                     preferred_element_type=_F32) + bg_ref[...]
        gates = 0.5 * jnp.tanh(0.5 * zg) + 0.5
        return (gates[:, :D_MODEL] * pa + gates[:, D_MODEL:] * pb).astype(_BF16)

    def residual(rows, merged):
        x1 = x_ref[rows, :] + jnp.dot(merged, wo_ref[...], preferred_element_type=_F32)
        return x1, _rms(x1, gm_ref[...]).astype(_BF16)

    def mlp(rows, x1, h):
        acc = x1
        for c in range(D_FF // D_MODEL):
            cols = slice(c * D_MODEL, (c + 1) * D_MODEL)
            u = jnp.maximum(jnp.dot(h, wu_ref[:, cols], preferred_element_type=_F32), 0.0)
            acc = acc + jnp.dot((u * u).astype(_BF16), wd_ref[cols, :], preferred_element_type=_F32)
        y_ref[rows, :] = _rms(acc, gf_ref[...])

    halves = [slice(i * (tm // 2), (i + 1) * (tm // 2)) for i in range(2)]
    merged = [branches(rows) for rows in halves]
    mixed = [residual(rows, m) for rows, m in zip(halves, merged)]
    for rows, (x1, h) in zip(halves, mixed):
        mlp(rows, x1, h)


def _tail(x, o_a, o_bs, lses, g_mix, wg, b_gate, wa, wb, wo, g_mlp, wu, wd, g_final, tm):
    b, s, _ = x.shape
    const = lambda bi, i: (0, 0)
    row = lambda bi, i: (bi, i, 0)
    resident = lambda a: pl.BlockSpec(a.shape, const, pipeline_mode=pl.Buffered(1))
    res_specs = [pl.BlockSpec((None, dil, tm // dil, QUAD), lambda bi, i: (bi, 0, i, 0))
                 for _, dil in B_GROUPS]
    in_specs = (
        [pl.BlockSpec((None, tm, D_MODEL), row), pl.BlockSpec((None, tm, A_Q), row)]
        + res_specs + res_specs
        + [resident(g_mix), resident(wg), resident(b_gate), resident(wa), resident(wb), resident(wo),
           resident(g_mlp), resident(wu), resident(wd), resident(g_final)]
    )
    stage = pltpu.VMEM((QUAD // LANES, tm, LANES), _F32)
    return pl.pallas_call(
        functools.partial(_tail_kernel, tm=tm),
        grid=(b, s // tm),
        in_specs=in_specs,
        out_specs=pl.BlockSpec((None, tm, D_MODEL), row),
        out_shape=jax.ShapeDtypeStruct((b, s, D_MODEL), _F32),
        scratch_shapes=[stage] * 6,
        compiler_params=pltpu.CompilerParams(
            dimension_semantics=("arbitrary", "arbitrary"), vmem_limit_bytes=VMEM_LIMIT_BYTES),
        name="tail",
    )(x, o_a, *o_bs, *lses, g_mix, wg, b_gate, wa, wb, wo, g_mlp, wu, wd, g_final)


def _prepare(rel_bias, g_mix, w_in, b_gate, w_branch_a, w_branch_b, w_out, attn_sink, g_mlp,
             w_up, w_down, g_final):
    scale = HEAD_DIM ** -0.5 * LOG2E
    order = np.asarray(A_HEAD_ORDER)
    o1, o2, o3 = A_Q, A_Q + A_KV, A_Q + 2 * A_KV
    o4, o5 = o3 + B_W, o3 + 2 * B_W
    o6 = o3 + 3 * B_W
    qa = w_in[:, :o1].reshape(D_MODEL, A_HEADS, HEAD_DIM)[:, order].reshape(D_MODEL, A_Q) * scale
    slabs = [qa, w_in[:, o1:o2], w_in[:, o2:o3]]
    for gi in range(len(B_GROUPS)):
        cols = slice(gi * B_OUT, (gi + 1) * B_OUT)
        slabs += [w_in[:, o3:o4][:, cols] * scale, w_in[:, o4:o5][:, cols], w_in[:, o5:o6][:, cols]]
    w_cat = jnp.concatenate(slabs, axis=1).astype(_BF16)

    wa = w_branch_a.reshape(A_HEADS, HEAD_DIM, D_MODEL)[order].reshape(A_Q, D_MODEL).astype(_BF16)
    bias_a = _bias_blocks(rel_bias[:, :A_HEADS][:, order], A_WINDOW, 1, Q_BLOCK + 2 * A_WINDOW)
    bias_b = []
    for gi, (window, dil) in enumerate(B_GROUPS):
        n = window // (2 * dil)
        table = rel_bias[:, A_HEADS + gi * B_HEADS_PER_GROUP:A_HEADS + (gi + 1) * B_HEADS_PER_GROUP]
        bias_b.append(_bias_blocks(table, n, dil, Q_BLOCK + 2 * n))
    return dict(
        w_cat=w_cat, wg=w_in[:, o6:].astype(_BF16), g_mix=g_mix.reshape(1, D_MODEL),
        b_gate=b_gate.reshape(1, 2 * D_MODEL),
        wa=wa, wb=w_branch_b.astype(_BF16), wo=w_out.astype(_BF16),
        sink=attn_sink[order].astype(_F32) * LOG2E, g_mlp=g_mlp.reshape(1, D_MODEL),
        wu=w_up.astype(_BF16), wd=w_down.astype(_BF16), g_final=g_final.reshape(1, D_MODEL),
        bias_a=bias_a, bias_b=bias_b,
    )


def _encode(x, p):
    qa, ka, va, zb1, zb2, zb3 = _inproj(x, p["g_mix"], p["w_cat"], tm=INPROJ_ROWS)
    o_a = _attn_a(qa, ka, va, p["bias_a"], p["sink"])
    o_bs, lses = [], []
    for gi, ((window, _), zr) in enumerate(zip(B_GROUPS, (zb1, zb2, zb3))):
        o, lse = _attn_b(zr, p["bias_b"][gi], window, f"attn_b{gi}")
        o_bs.append(o)
        lses.append(lse)
    return _tail(x, o_a, o_bs, lses, p["g_mix"], p["wg"], p["b_gate"], p["wa"], p["wb"], p["wo"],
                 p["g_mlp"], p["wu"], p["wd"], p["g_final"], tm=TAIL_ROWS)


def kernel(x_prompt, x_sample, rel_bias, g_mix, w_in, b_gate, w_branch_a, w_branch_b, w_out,
           attn_sink, g_mlp, w_up, w_down, g_final):
    assert g_mix.shape[0] == 1, "single-layer encoder"
    p = _prepare(rel_bias, g_mix[0], w_in[0], b_gate[0], w_branch_a[0], w_branch_b[0], w_out[0],
                 attn_sink[0], g_mlp[0], w_up[0], w_down[0], g_final)
    return (_encode(x_prompt, p), _encode(x_sample, p))
```

```python
import functools
import math

import numpy as np
import jax
import jax.numpy as jnp
from jax import lax
from jax.experimental import pallas as pl
from jax.experimental.pallas import tpu as pltpu

D_MODEL = 1024
HEAD_DIM = 64
A_HEADS = 8
A_KV_HEADS = 2
A_WINDOW = 128
B_GROUPS = ((128, 1), (512, 4), (2048, 16))
B_HEADS_PER_GROUP = 4
B_HEADS = B_HEADS_PER_GROUP * len(B_GROUPS)
REL_BUCKETS = 32
REL_MAX_DISTANCE = 1024
D_FF = 4 * D_MODEL
A_Q = A_HEADS * HEAD_DIM
A_KV = A_KV_HEADS * HEAD_DIM
B_W = B_HEADS * HEAD_DIM
B_OUT = B_HEADS_PER_GROUP * HEAD_DIM
RMS_EPS = 1e-6
NEG_INF = -1e30
LOG2E = math.log2(math.e)
LN2 = math.log(2.0)

LANES = 128
QUAD = 4 * HEAD_DIM
PAIR = 2 * HEAD_DIM
SLAB = 3 * QUAD
Q_BLOCK = 128
FOLD = Q_BLOCK // 2
INPROJ_ROWS = 1024
SUB_ROWS = 256
RELAYOUT_STRIDE = 4
TAIL_ROWS = 512
UNITS_PER_STEP = 32
DILATED_CHUNK_ROWS = 1024
A_CHUNK_ROWS = 4096
VMEM_LIMIT_BYTES = 58 * 1024 * 1024

A_HEAD_ORDER = (0, 4, 1, 5, 2, 6, 3, 7)

_BF16 = jnp.bfloat16
_F32 = jnp.float32


def _rel_bucket(rel):
    half = REL_BUCKETS // 2
    max_exact = half // 2
    n = np.abs(rel)
    large = max_exact + (np.log(np.maximum(n, 1) / max_exact)
                         / np.log(REL_MAX_DISTANCE / max_exact) * (half - max_exact)).astype(np.int32)
    large = np.minimum(large, half - 1)
    return (np.where(rel > 0, half, 0) + np.where(n < max_exact, n, large)).astype(np.int32)


def _bias_blocks(table, n, dist_scale, nk):
    rel = np.arange(nk)[:, None] - n - np.arange(Q_BLOCK)[None, :]
    bucket = _rel_bucket(rel * dist_scale)
    table = table.astype(_F32) * LOG2E
    bias = jnp.zeros((table.shape[1], nk, Q_BLOCK), _F32)
    for bkt in np.unique(bucket):
        bias = jnp.where(jnp.asarray(bucket == bkt)[None], table[bkt][:, None, None], bias)
    band = np.abs(rel) <= n
    key = np.arange(nk)[:, None]
    variants = []
    for var in range(4):
        ok = band.copy()
        if var & 1:
            ok &= key >= n
        if var & 2:
            ok &= key < nk - n
        variants.append(jnp.where(ok[None], bias, NEG_INF))
    tiles = jnp.stack(variants)
    return jnp.concatenate(
        [jnp.where(np.arange(Q_BLOCK) < FOLD, tiles[..., :FOLD, :], tiles[..., nk - FOLD:, :]),
         tiles[..., FOLD:nk - FOLD, :]], axis=-2)


def _inproj_kernel(x_ref, g_ref, w_ref, qa_ref, ka_ref, va_ref, zb1_ref, zb2_ref, zb3_ref, zs_ref,
                   zt_ref, *, tm):
    x = x_ref[...]
    ms = jnp.mean(x * x, axis=-1, keepdims=True)
    h = (x * lax.rsqrt(ms + RMS_EPS) * g_ref[...]).astype(_BF16)
    for sub in range(tm // SUB_ROWS):
        rows = slice(sub * SUB_ROWS, (sub + 1) * SUB_ROWS)
        hs = h[rows]
        for gi in reversed(range(len(B_GROUPS))):
            out, dil = (zb1_ref, zb2_ref, zb3_ref)[gi], B_GROUPS[gi][1]
            z = jnp.dot(hs, w_ref[:, (gi + 1) * SLAB:(gi + 2) * SLAB], preferred_element_type=_F32)
            if dil == 1:
                for part in range(3):
                    out[0, part, rows, :] = z[:, part * QUAD:(part + 1) * QUAD].astype(_BF16)
                continue
            for k in range(SLAB // LANES):
                zs_ref[k, rows, :] = z[:, k * LANES:(k + 1) * LANES]
            src_ref, base, stride, per = zs_ref, sub * SUB_ROWS, dil, SUB_ROWS // dil
            if dil == RELAYOUT_STRIDE ** 2:
                quarter, stride = SUB_ROWS // RELAYOUT_STRIDE, RELAYOUT_STRIDE
                for k in range(SLAB // LANES):
                    for r in range(stride):
                        zt_ref[k, r * quarter:(r + 1) * quarter, :] = (
                            zs_ref[k, pl.ds(base + r, quarter, stride=stride), :])
                src_ref, base = zt_ref, 0
            for r in range(dil):
                start = base + r if stride == dil else (r % stride) * quarter + r // stride
                for k in range(SLAB // LANES):
                    part, half = divmod(k, QUAD // LANES)
                    out[r, part, sub * per:(sub + 1) * per, half * LANES:(half + 1) * LANES] = (
                        src_ref[k, pl.ds(start, per, stride=stride), :].astype(_BF16))
        za = jnp.dot(hs, w_ref[:, :SLAB], preferred_element_type=_F32).astype(_BF16)
        qa_ref[rows, :] = za[:, :A_Q]
        ka_ref[rows, :] = za[:, A_Q:A_Q + A_KV]
        va_ref[rows, :] = za[:, A_Q + A_KV:]


def _inproj(x, g_mix, w_cat, tm):
    b, s, _ = x.shape
    const = lambda bi, i: (0, 0)
    row = lambda bi, i: (bi, i, 0)
    res_specs = [pl.BlockSpec((None, dil, 3, tm // dil, QUAD), lambda bi, i: (bi, 0, 0, i, 0))
                 for _, dil in B_GROUPS]
    res_shapes = [jax.ShapeDtypeStruct((b, dil, 3, s // dil, QUAD), _BF16) for _, dil in B_GROUPS]
    a_widths = (A_Q, A_KV, A_KV)
    return pl.pallas_call(
        functools.partial(_inproj_kernel, tm=tm),
        grid=(b, s // tm),
        in_specs=[
            pl.BlockSpec((None, tm, D_MODEL), row),
            pl.BlockSpec((1, D_MODEL), const),
            pl.BlockSpec(w_cat.shape, const, pipeline_mode=pl.Buffered(1)),
        ],
        out_specs=[pl.BlockSpec((None, tm, w), row) for w in a_widths] + res_specs,
        out_shape=[jax.ShapeDtypeStruct((b, s, w), _BF16) for w in a_widths] + res_shapes,
        scratch_shapes=[pltpu.VMEM((SLAB // LANES, tm, LANES), _F32),
                        pltpu.VMEM((SLAB // LANES, SUB_ROWS, LANES), _F32)],
        compiler_params=pltpu.CompilerParams(
            dimension_semantics=("arbitrary", "arbitrary"), vmem_limit_bytes=VMEM_LIMIT_BYTES),
        name="inproj",
    )(x, g_mix, w_cat)


def _attn_ops(q_ref, kv_refs, bias_ref, sink_ref, stage_refs, write, *, halo, lc, kv_pairs,
              first, last, want_lse):
    kc_ref, vc_ref, kp_ref, vp_ref, kn_ref, vn_ref = kv_refs
    kext_ref, vext_ref, vt_ref = stage_refs
    nk = Q_BLOCK + 2 * halo
    nblk = lc // Q_BLOCK
    low_queries = lax.broadcasted_iota(jnp.int32, (FOLD, Q_BLOCK), 1) < FOLD

    def stage(r):
        pieces = ((0, halo, kp_ref, vp_ref), (halo, lc, kc_ref, vc_ref),
                  (halo + lc, halo, kn_ref, vn_ref))
        for lo, rows, k_src, v_src in pieces:
            k = k_src[r]
            kext_ref[r, lo:lo + rows, :] = k if kv_pairs == 2 else jnp.concatenate([k, k], axis=-1)
            vext_ref[r, lo:lo + rows, :] = v_src[r]
        vt_ref[r] = vext_ref[r].astype(_F32).T.astype(_BF16)

    def scores(r, i, quad):
        if i == 0 and quad == 0:
            stage(r)
        r0 = i * Q_BLOCK
        q4 = q_ref[r, r0:r0 + Q_BLOCK, quad * QUAD:(quad + 1) * QUAD]
        qt = q4.astype(_F32).T.astype(_BF16)
        cols = []
        for j in range(4):
            slabs = [qt[j * HEAD_DIM:(j + 1) * HEAD_DIM]]
            if j:
                slabs.insert(0, jnp.zeros((j * HEAD_DIM, Q_BLOCK), _BF16))
            if j < 3:
                slabs.append(jnp.zeros((QUAD - (j + 1) * HEAD_DIM, Q_BLOCK), _BF16))
            cols.append(jnp.concatenate(slabs, axis=0))
        return jnp.dot(kext_ref[r, r0:r0 + nk, :], jnp.concatenate(cols, axis=1),
                       preferred_element_type=_F32)

    def finish(r, i, quad, st):
        r0 = i * Q_BLOCK
        var = (first if i == 0 else 0) + (last if i == nblk - 1 else 0)
        pts, scales, lses = [], [], []
        for j in range(4):
            head = quad * 4 + j
            s = st[:, j * Q_BLOCK:(j + 1) * Q_BLOCK]
            s = jnp.concatenate(
                [jnp.where(low_queries, s[:FOLD], s[nk - FOLD:]), s[FOLD:nk - FOLD]], axis=0)
            logits = s + bias_ref[var, head]
            m = jnp.max(logits, axis=0, keepdims=True)
            if sink_ref is not None:
                sink = sink_ref[head]
                m = jnp.maximum(m, sink)
            p = jnp.exp2(logits - m)
            den = jnp.sum(p, axis=0, keepdims=True)
            if sink_ref is not None:
                den = den + jnp.exp2(sink - m)
            shared, zero = p[:FOLD], jnp.zeros((FOLD, Q_BLOCK), _F32)
            p = jnp.concatenate([jnp.where(low_queries, shared, zero), p[FOLD:],
                                 jnp.where(low_queries, zero, shared)], axis=0)
            pts.append(p.astype(_BF16))
            scales.append(1.0 / den)
            lses.append(m * LN2 + jnp.log(den))
        outs = []
        for pair in range(2):
            j0 = 2 * pair
            vrows = (pair % kv_pairs) * PAIR
            ot = jnp.dot(vt_ref[r, vrows:vrows + PAIR, r0:r0 + nk],
                         jnp.concatenate(pts[j0:j0 + 2], axis=1), preferred_element_type=_F32)
            outs.append(ot[:HEAD_DIM, :Q_BLOCK] * scales[j0])
            outs.append(ot[HEAD_DIM:, Q_BLOCK:] * scales[j0 + 1])
        lse = None
        if want_lse:
            lse = jnp.concatenate(
                [jnp.broadcast_to(l, (HEAD_DIM, Q_BLOCK)) for l in lses], axis=0).T
        write(r, r0, quad, jnp.concatenate(outs, axis=0).T.astype(_BF16), lse)

    return scores, finish


def _attn_kernel(*refs, halo, lc, nres, nquad, kv_pairs, has_sink, want_lse, chunk_axis):
    it = iter(refs)
    q_ref = next(it)
    kv_refs = [next(it) for _ in range(6)]
    bias_ref = next(it)
    sink_ref = next(it) if has_sink else None
    o_ref = next(it)
    lse_ref = next(it) if want_lse else None
    stage_refs = [next(it) for _ in range(3)]

    c = pl.program_id(chunk_axis)
    first = jnp.where(c == 0, 1, 0)
    last = jnp.where(c == pl.num_programs(chunk_axis) - 1, 2, 0)

    def write(r, r0, quad, o, lse):
        o_ref[r, r0:r0 + Q_BLOCK, quad * QUAD:(quad + 1) * QUAD] = o
        if want_lse:
            lse_ref[r, r0:r0 + Q_BLOCK, :] = lse

    scores, finish = _attn_ops(q_ref, kv_refs, bias_ref, sink_ref, stage_refs, write, halo=halo,
                               lc=lc, kv_pairs=kv_pairs, first=first, last=last, want_lse=want_lse)

    units = [(r, i, quad) for r in range(nres) for i in range(lc // Q_BLOCK) for quad in range(nquad)]
    ahead = 2 if nquad == 1 else 1
    pending = [scores(*u) for u in units[:ahead]]
    for idx, unit in enumerate(units):
        if idx + ahead < len(units):
            pending.append(scores(*units[idx + ahead]))
        finish(*unit, pending.pop(0))


def _attn_call(q, k, v, bias, sink, *, halo, lc, nres, want_lse, name):
    def operand(x):
        arr, part = x if isinstance(x, tuple) else (x, None)
        width = arr.shape[-1]

        def spec(rows, row_block):
            if part is None:
                return pl.BlockSpec((None, nres, rows, width),
                                    lambda bi, r, c: (bi, r, row_block(c), 0))
            return pl.BlockSpec((None, nres, None, rows, width),
                                lambda bi, r, c: (bi, r, part, row_block(c), 0))
        return arr, spec

    (q_arr, q_spec), (k_arr, k_spec), (v_arr, v_spec) = operand(q), operand(k), operand(v)
    b, dil, length, qw, kvw = q_arr.shape[0], q_arr.shape[1], q_arr.shape[-2], q_arr.shape[-1], k_arr.shape[-1]
    per_chunk = lc // halo
    ext = lc + 2 * halo
    here = lambda c: c
    prev = lambda c: jnp.maximum(c * per_chunk - 1, 0)
    nxt = lambda c: jnp.minimum((c + 1) * per_chunk, length // halo - 1)
    in_specs = [
        q_spec(lc, here), k_spec(lc, here), v_spec(lc, here),
        k_spec(halo, prev), v_spec(halo, prev), k_spec(halo, nxt), v_spec(halo, nxt),
        pl.BlockSpec(bias.shape, lambda bi, r, c: (0, 0, 0, 0), pipeline_mode=pl.Buffered(1)),
    ]
    args = [q_arr, k_arr, v_arr, k_arr, v_arr, k_arr, v_arr, bias]
    if sink is not None:
        in_specs.append(pl.BlockSpec(memory_space=pltpu.SMEM))
        args.append(sink)
    out_spec = pl.BlockSpec((None, nres, lc, qw), lambda bi, r, c: (bi, r, c, 0))
    out_specs, out_shape = [out_spec], [jax.ShapeDtypeStruct((b, dil, length, qw), _BF16)]
    if want_lse:
        out_specs.append(out_spec)
        out_shape.append(jax.ShapeDtypeStruct((b, dil, length, qw), _F32))
    kernel = functools.partial(
        _attn_kernel, halo=halo, lc=lc, nres=nres, nquad=qw // QUAD, kv_pairs=kvw // PAIR,
        has_sink=sink is not None, want_lse=want_lse, chunk_axis=2)
    grid = (b, dil // nres, length // lc)
    return pl.pallas_call(
        kernel,
        grid=grid,
        in_specs=in_specs,
        out_specs=out_specs,
        out_shape=out_shape,
        scratch_shapes=[
            pltpu.VMEM((nres, ext, QUAD), _BF16),
            pltpu.VMEM((nres, ext, kvw), _BF16),
            pltpu.VMEM((nres, kvw, ext), _BF16),
        ],
        compiler_params=pltpu.CompilerParams(
            dimension_semantics=("arbitrary",) * len(grid), vmem_limit_bytes=VMEM_LIMIT_BYTES),
        name=name,
    )(*args)


def _attn_a(qa, ka, va, bias, sink):
    b, s, _ = qa.shape
    lc = min(s, A_CHUNK_ROWS)
    seq = lambda a: a.reshape(b, 1, s, a.shape[-1])
    (o,) = _attn_call(seq(qa), seq(ka), seq(va), bias, sink, halo=A_WINDOW, lc=lc, nres=1,
                      want_lse=False, name="attn_a")
    return o.reshape(b, s, A_Q)


def _attn_b(zr, bias, window, name):
    _, dil, _, length, _ = zr.shape
    lc = min(length, DILATED_CHUNK_ROWS if dil > 1 else Q_BLOCK * UNITS_PER_STEP)
    nres = min(dil, UNITS_PER_STEP // (lc // Q_BLOCK))
    return _attn_call((zr, 0), (zr, 1), (zr, 2), bias, None, halo=window // (2 * dil), lc=lc,
                      nres=nres, want_lse=True, name=name)


def _rms(x, g):
    return x * lax.rsqrt(jnp.mean(x * x, axis=-1, keepdims=True) + RMS_EPS) * g


def _tokens(src_ref, stage_ref, tmp_ref, tm):
    dil = src_ref.shape[0]
    if dil == 1:
        return src_ref[0].astype(_F32)
    lane_slabs = range(QUAD // LANES)
    if dil == RELAYOUT_STRIDE ** 2:
        per, quarter = tm // dil, tm // RELAYOUT_STRIDE
        for r in range(dil):
            start = (r % RELAYOUT_STRIDE) * quarter + r // RELAYOUT_STRIDE
            for k in lane_slabs:
                tmp_ref[k, pl.ds(start, per, stride=RELAYOUT_STRIDE), :] = (
                    src_ref[r, :, k * LANES:(k + 1) * LANES].astype(_F32))
        for r in range(RELAYOUT_STRIDE):
            for k in lane_slabs:
                stage_ref[k, pl.ds(r, quarter, stride=RELAYOUT_STRIDE), :] = (
                    tmp_ref[k, r * quarter:(r + 1) * quarter, :])
    else:
        for r in range(dil):
            for k in lane_slabs:
                stage_ref[k, pl.ds(r, tm // dil, stride=dil), :] = (
                    src_ref[r, :, k * LANES:(k + 1) * LANES].astype(_F32))
    return jnp.concatenate([stage_ref[k] for k in lane_slabs], axis=-1)


def _tail_kernel(x_ref, oa_ref, ob1_ref, ob2_ref, ob3_ref, l1_ref, l2_ref, l3_ref,
                 gx_ref, wg_ref, bg_ref, wa_ref, wb_ref, wo_ref, gm_ref, wu_ref, wd_ref, gf_ref, y_ref,
                 so2_ref, so3_ref, sl2_ref, sl3_ref, to3_ref, tl3_ref, *, tm):
    outs = [_tokens(ob1_ref, None, None, tm), _tokens(ob2_ref, so2_ref, None, tm),
            _tokens(ob3_ref, so3_ref, to3_ref, tm)]
    lses = [_tokens(l1_ref, None, None, tm), _tokens(l2_ref, sl2_ref, None, tm),
            _tokens(l3_ref, sl3_ref, tl3_ref, tm)]
    top = jnp.maximum(jnp.maximum(lses[0], lses[1]), lses[2])
    ws = [jnp.exp(l - top) for l in lses]
    mix = ws[0] * outs[0] + ws[1] * outs[1] + ws[2] * outs[2]
    o_b = (mix / (ws[0] + ws[1] + ws[2])).astype(_BF16)

    def branches(rows):
        pa = jnp.dot(oa_ref[rows, :], wa_ref[...], preferred_element_type=_F32)
        pb = jnp.dot(o_b[rows], wb_ref[...], preferred_element_type=_F32)
        zg = jnp.dot(_rms(x_ref[rows, :], gx_ref[...]).astype(_BF16), wg_ref[...],
                     preferred_element_type=_F32) + bg_ref[...]
        gates = 0.5 * jnp.tanh(0.5 * zg) + 0.5
        return (gates[:, :D_MODEL] * pa + gates[:, D_MODEL:] * pb).astype(_BF16)

    def residual(rows, merged):
        x1 = x_ref[rows, :] + jnp.dot(merged, wo_ref[...], preferred_element_type=_F32)
        return x1, _rms(x1, gm_ref[...]).astype(_BF16)

    def mlp(rows, x1, h):
        acc = x1
        for c in range(D_FF // D_MODEL):
            cols = slice(c * D_MODEL, (c + 1) * D_MODEL)
            u = jnp.maximum(jnp.dot(h, wu_ref[:, cols], preferred_element_type=_F32), 0.0)
            acc = acc + jnp.dot((u * u).astype(_BF16), wd_ref[cols, :], preferred_element_type=_F32)
        y_ref[rows, :] = _rms(acc, gf_ref[...])

    halves = [slice(i * (tm // 2), (i + 1) * (tm // 2)) for i in range(2)]
    merged = [branches(rows) for rows in halves]
    mixed = [residual(rows, m) for rows, m in zip(halves, merged)]
    for rows, (x1, h) in zip(halves, mixed):
        mlp(rows, x1, h)


def _tail(x, o_a, o_bs, lses, g_mix, wg, b_gate, wa, wb, wo, g_mlp, wu, wd, g_final, tm):
    b, s, _ = x.shape
    const = lambda bi, i: (0, 0)
    row = lambda bi, i: (bi, i, 0)
    resident = lambda a: pl.BlockSpec(a.shape, const, pipeline_mode=pl.Buffered(1))
    res_specs = [pl.BlockSpec((None, dil, tm // dil, QUAD), lambda bi, i: (bi, 0, i, 0))
                 for _, dil in B_GROUPS]
    in_specs = (
        [pl.BlockSpec((None, tm, D_MODEL), row), pl.BlockSpec((None, tm, A_Q), row)]
        + res_specs + res_specs
        + [resident(g_mix), resident(wg), resident(b_gate), resident(wa), resident(wb), resident(wo),
           resident(g_mlp), resident(wu), resident(wd), resident(g_final)]
    )
    stage = pltpu.VMEM((QUAD // LANES, tm, LANES), _F32)
    return pl.pallas_call(
        functools.partial(_tail_kernel, tm=tm),
        grid=(b, s // tm),
        in_specs=in_specs,
        out_specs=pl.BlockSpec((None, tm, D_MODEL), row),
        out_shape=jax.ShapeDtypeStruct((b, s, D_MODEL), _F32),
        scratch_shapes=[stage] * 6,
        compiler_params=pltpu.CompilerParams(
            dimension_semantics=("arbitrary", "arbitrary"), vmem_limit_bytes=VMEM_LIMIT_BYTES),
        name="tail",
    )(x, o_a, *o_bs, *lses, g_mix, wg, b_gate, wa, wb, wo, g_mlp, wu, wd, g_final)


def _prepare(rel_bias, g_mix, w_in, b_gate, w_branch_a, w_branch_b, w_out, attn_sink, g_mlp,
             w_up, w_down, g_final):
    scale = HEAD_DIM ** -0.5 * LOG2E
    order = np.asarray(A_HEAD_ORDER)
    o1, o2, o3 = A_Q, A_Q + A_KV, A_Q + 2 * A_KV
    o4, o5 = o3 + B_W, o3 + 2 * B_W
    o6 = o3 + 3 * B_W
    qa = w_in[:, :o1].reshape(D_MODEL, A_HEADS, HEAD_DIM)[:, order].reshape(D_MODEL, A_Q) * scale
    slabs = [qa, w_in[:, o1:o2], w_in[:, o2:o3]]
    for gi in range(len(B_GROUPS)):
        cols = slice(gi * B_OUT, (gi + 1) * B_OUT)
        slabs += [w_in[:, o3:o4][:, cols] * scale, w_in[:, o4:o5][:, cols], w_in[:, o5:o6][:, cols]]
    w_cat = jnp.concatenate(slabs, axis=1).astype(_BF16)

    wa = w_branch_a.reshape(A_HEADS, HEAD_DIM, D_MODEL)[order].reshape(A_Q, D_MODEL).astype(_BF16)
    bias_a = _bias_blocks(rel_bias[:, :A_HEADS][:, order], A_WINDOW, 1, Q_BLOCK + 2 * A_WINDOW)
    bias_b = []
    for gi, (window, dil) in enumerate(B_GROUPS):
        n = window // (2 * dil)
        table = rel_bias[:, A_HEADS + gi * B_HEADS_PER_GROUP:A_HEADS + (gi + 1) * B_HEADS_PER_GROUP]
        bias_b.append(_bias_blocks(table, n, dil, Q_BLOCK + 2 * n))
    return dict(
        w_cat=w_cat, wg=w_in[:, o6:].astype(_BF16), g_mix=g_mix.reshape(1, D_MODEL),
        b_gate=b_gate.reshape(1, 2 * D_MODEL),
        wa=wa, wb=w_branch_b.astype(_BF16), wo=w_out.astype(_BF16),
        sink=attn_sink[order].astype(_F32) * LOG2E, g_mlp=g_mlp.reshape(1, D_MODEL),
        wu=w_up.astype(_BF16), wd=w_down.astype(_BF16), g_final=g_final.reshape(1, D_MODEL),
        bias_a=bias_a, bias_b=bias_b,
    )


def _encode(x, p):
    qa, ka, va, zb1, zb2, zb3 = _inproj(x, p["g_mix"], p["w_cat"], tm=INPROJ_ROWS)
    o_a = _attn_a(qa, ka, va, p["bias_a"], p["sink"])
    o_bs, lses = [], []
    for gi, ((window, _), zr) in enumerate(zip(B_GROUPS, (zb1, zb2, zb3))):
        o, lse = _attn_b(zr, p["bias_b"][gi], window, f"attn_b{gi}")
        o_bs.append(o)
        lses.append(lse)
    return _tail(x, o_a, o_bs, lses, p["g_mix"], p["wg"], p["b_gate"], p["wa"], p["wb"], p["wo"],
                 p["g_mlp"], p["wu"], p["wd"], p["g_final"], tm=TAIL_ROWS)


def kernel(x_prompt, x_sample, rel_bias, g_mix, w_in, b_gate, w_branch_a, w_branch_b, w_out,
           attn_sink, g_mlp, w_up, w_down, g_final):
    assert g_mix.shape[0] == 1, "single-layer encoder"
    p = _prepare(rel_bias, g_mix[0], w_in[0], b_gate[0], w_branch_a[0], w_branch_b[0], w_out[0],
                 attn_sink[0], g_mlp[0], w_up[0], w_down[0], g_final)
    return (_encode(x_prompt, p), _encode(x_sample, p))
```
